```python
import jax, jax.numpy as jnp
from jax import lax
import numpy as np

D_MODEL = 1024
BATCH = 2
SEQ = 8192
DEPTH = 2

NSA_HEADS = 8
NSA_GROUPS = 2
NSA_HPG = NSA_HEADS // NSA_GROUPS
NSA_DK = 64
NSA_DV = 64
CMP_BLOCK = 32
CMP_STRIDE = 16
SLC_BLOCK = 64
N_SELECT = 16
WINDOW = 512
MLA_HEADS = 8
MLA_NOPE = 64
MLA_ROPE = 32
MLA_V = 64
MLA_Q_RANK = 384
MLA_KV_RANK = 256
ROPE_THETA = 10000.0
D_FF = 2816
N_MOD = 9
Q_BLOCK = 128
NORM_EPS = 1e-6

IN_SPLITS = (
    NSA_HEADS * NSA_DK,
    NSA_GROUPS * NSA_DK,
    NSA_GROUPS * NSA_DV,
    NSA_GROUPS * NSA_DK,
    NSA_GROUPS * NSA_DV,
    NSA_GROUPS * NSA_DK,
    NSA_GROUPS * NSA_DV,
    3 * NSA_HEADS,
    MLA_Q_RANK,
    MLA_KV_RANK,
    MLA_ROPE,
    2 * D_MODEL,
)
IN_TOTAL = sum(IN_SPLITS)

kernel_name = "hybrid_nsa_mla_macaron_adaln"


def rms_norm(x):
    xf = x.astype(jnp.float32)
    return (xf * lax.rsqrt(jnp.mean(xf * xf, -1, keepdims=True) + NORM_EPS)).astype(x.dtype)


def modulate(x, shift, scale):
    return rms_norm(x) * (1 + scale[:, None, :]) + shift[:, None, :]


def swiglu(u, w_gate, w_up, w_down):
    return (jax.nn.silu(u @ w_gate) * (u @ w_up)) @ w_down


def masked_softmax(scores, mask):
    s = jnp.where(mask, scores.astype(jnp.float32), -jnp.inf)
    m = jnp.max(s, -1, keepdims=True)
    m = jnp.where(jnp.isfinite(m), m, 0.0)
    p = jnp.exp(s - m)
    return p / jnp.maximum(jnp.sum(p, -1, keepdims=True), 1.0)


def rope(x, cos, sin):
    half = x.shape[-1] // 2
    x1, x2 = x[..., :half], x[..., half:]
    return jnp.concatenate([x1 * cos - x2 * sin, x1 * sin + x2 * cos], -1).astype(x.dtype)


def alibi_slopes(n):
    return 2.0 ** (-8.0 * jnp.arange(1, n + 1, dtype=jnp.float32) / n)


def compress(k, w1, w2, pe):
    B, G, S, d = k.shape
    n_chunks = S // CMP_STRIDE
    r = CMP_BLOCK // CMP_STRIDE
    n_cmp = n_chunks - r + 1
    chunks = k.reshape(B, G, n_chunks, CMP_STRIDE, d)
    blocks = jnp.concatenate([chunks[:, :, j:j + n_cmp] for j in range(r)], axis=3)
    blocks = (blocks + pe).reshape(B, G, n_cmp, CMP_BLOCK * d)
    return jax.nn.gelu(blocks @ w1) @ w2


def nsa_attention(q, k_c, v_c, k_s, v_s, k_w, v_w, gates):
    B, G, HPG, S, dk = q.shape
    dv = v_c.shape[-1]
    nq = S // Q_BLOCK
    n_slc = S // SLC_BLOCK
    n_top = min(N_SELECT, n_slc)
    n_cmp = k_c.shape[2]
    ratio = SLC_BLOCK // CMP_STRIDE
    r = CMP_BLOCK // CMP_STRIDE
    cmp_end = jnp.arange(n_cmp) * CMP_STRIDE + CMP_BLOCK - 1
    ks_blocks = k_s.reshape(B, G, n_slc, SLC_BLOCK, dk)
    vs_blocks = v_s.reshape(B, G, n_slc, SLC_BLOCK, dv)
    kw_pad = jnp.pad(k_w, ((0, 0), (0, 0), (WINDOW, 0), (0, 0)))
    vw_pad = jnp.pad(v_w, ((0, 0), (0, 0), (WINDOW, 0), (0, 0)))
    slope = alibi_slopes(NSA_HEADS).reshape(1, G, HPG, 1, 1)
    scale = dk ** -0.5
    b_ix = jnp.arange(B)[:, None, None, None]
    g_ix = jnp.arange(G)[None, :, None, None]
    blk = jnp.arange(n_slc)

    def block(i):
        q0 = i * Q_BLOCK
        qb = lax.dynamic_slice_in_dim(q, q0, Q_BLOCK, axis=3)
        gb = lax.dynamic_slice_in_dim(gates, q0, Q_BLOCK, axis=3)
        t = q0 + jnp.arange(Q_BLOCK)
        d_c = (t[:, None] - cmp_end[None, :]).astype(jnp.float32)
        s_c = jnp.einsum('bghqd,bgnd->bghqn', qb, k_c).astype(jnp.float32) * scale - slope * d_c
        p_c = masked_softmax(s_c, cmp_end[None, :] <= t[:, None])
        o_c = jnp.einsum('bghqn,bgnd->bghqd', p_c.astype(v_c.dtype), v_c)
        imp = jnp.pad(p_c.sum(axis=2), ((0, 0), (0, 0), (0, 0), (r - 1, n_slc * ratio - n_cmp)))
        imp_s = imp[..., 0:ratio * n_slc:ratio]
        for j in range(1, ratio + r - 1):
            imp_s = imp_s + imp[..., j:j + ratio * n_slc:ratio]
        cur = t // SLC_BLOCK
        forced = (blk[None, :] == 0) | (blk[None, :] == cur[:, None]) | (blk[None, :] == cur[:, None] - 1)
        valid = blk[None, :] * SLC_BLOCK <= t[:, None]
        imp_s = jnp.where(forced, jnp.inf, jnp.where(valid, imp_s, -jnp.inf))
        _, idx = lax.top_k(imp_s, n_top)
        ks = ks_blocks[b_ix, g_ix, idx].reshape(B, G, Q_BLOCK, n_top * SLC_BLOCK, dk)
        vs = vs_blocks[b_ix, g_ix, idx].reshape(B, G, Q_BLOCK, n_top * SLC_BLOCK, dv)
        pos_s = (idx[..., None] * SLC_BLOCK + jnp.arange(SLC_BLOCK)).reshape(B, G, Q_BLOCK, n_top * SLC_BLOCK)
        d_s = (t[:, None] - pos_s).astype(jnp.float32)[:, :, None]
        s_s = jnp.einsum('bghqd,bgqkd->bghqk', qb, ks).astype(jnp.float32) * scale - slope * d_s
        p_s = masked_softmax(s_s, (pos_s <= t[:, None])[:, :, None])
        o_s = jnp.einsum('bghqk,bgqkd->bghqd', p_s.astype(vs.dtype), vs)
        kw = lax.dynamic_slice_in_dim(kw_pad, q0, WINDOW + Q_BLOCK, axis=2)
        vw = lax.dynamic_slice_in_dim(vw_pad, q0, WINDOW + Q_BLOCK, axis=2)
        pos_w = q0 - WINDOW + jnp.arange(WINDOW + Q_BLOCK)
        d_w = t[:, None] - pos_w[None, :]
        mask_w = (d_w >= 0) & (d_w < WINDOW) & (pos_w[None, :] >= 0)
        s_w = jnp.einsum('bghqd,bgkd->bghqk', qb, kw).astype(jnp.float32) * scale - slope * d_w.astype(jnp.float32)
        p_w = masked_softmax(s_w, mask_w)
        o_w = jnp.einsum('bghqk,bgkd->bghqd', p_w.astype(vw.dtype), vw)
        return gb[..., 0:1] * o_c + gb[..., 1:2] * o_s + gb[..., 2:3] * o_w

    out = lax.map(block, jnp.arange(nq))
    return out.transpose(1, 0, 4, 2, 3, 5).reshape(B, S, G * HPG * dv)


def mla_attention(q_nope, q_rope, k_nope, k_rope, v):
    B, H, S, _ = q_nope.shape
    dv = v.shape[-1]
    nq = S // Q_BLOCK
    scale = (MLA_NOPE + MLA_ROPE) ** -0.5
    kpos = jnp.arange(S)

    def block(i):
        q0 = i * Q_BLOCK
        qn = lax.dynamic_slice_in_dim(q_nope, q0, Q_BLOCK, axis=2)
        qr = lax.dynamic_slice_in_dim(q_rope, q0, Q_BLOCK, axis=2)
        s = (jnp.einsum('bhqd,bhkd->bhqk', qn, k_nope)
             + jnp.einsum('bhqd,bkd->bhqk', qr, k_rope)).astype(jnp.float32) * scale
        qpos = q0 + jnp.arange(Q_BLOCK)
        p = masked_softmax(s, kpos[None, :] <= qpos[:, None])
        return jnp.einsum('bhqk,bhkd->bhqd', p.astype(v.dtype), v)

    out = lax.map(block, jnp.arange(nq))
    return out.transpose(1, 0, 3, 2, 4).reshape(B, S, H * dv)


def token_mix(u, cos, sin, w_in, cmpk_w1, cmpk_w2, cmpk_pe, cmpv_w1, cmpv_w2, cmpv_pe,
              mla_q_norm, mla_w_uq, mla_kv_norm, mla_w_uk, mla_w_uv,
              w_branch_a, w_branch_b, w_out):
    B, S, _ = u.shape
    points = np.cumsum(IN_SPLITS)[:-1].tolist()
    (q_a, kc, vc, ks, vs, kw, vw, g_nsa, cq, ckv, kr, g_merge) = jnp.split(u @ w_in, points, axis=-1)
    q_a = q_a.reshape(B, S, NSA_GROUPS, NSA_HPG, NSA_DK).transpose(0, 2, 3, 1, 4)
    grp = lambda t: t.reshape(B, S, NSA_GROUPS, -1).transpose(0, 2, 1, 3)
    g_nsa = jax.nn.sigmoid(g_nsa).reshape(B, S, NSA_GROUPS, NSA_HPG, 3).transpose(0, 2, 3, 1, 4)
    k_c = compress(grp(kc), cmpk_w1, cmpk_w2, cmpk_pe)
    v_c = compress(grp(vc), cmpv_w1, cmpv_w2, cmpv_pe)
    o_a = nsa_attention(q_a, k_c, v_c, grp(ks), grp(vs), grp(kw), grp(vw), g_nsa)
    q = (rms_norm(cq) * mla_q_norm) @ mla_w_uq
    q = q.reshape(B, S, MLA_HEADS, MLA_NOPE + MLA_ROPE)
    q_nope = q[..., :MLA_NOPE].transpose(0, 2, 1, 3)
    q_rope = rope(q[..., MLA_NOPE:], cos[:, None, :], sin[:, None, :]).transpose(0, 2, 1, 3)
    ckv = rms_norm(ckv) * mla_kv_norm
    k_nope = (ckv @ mla_w_uk).reshape(B, S, MLA_HEADS, MLA_NOPE).transpose(0, 2, 1, 3)
    v = (ckv @ mla_w_uv).reshape(B, S, MLA_HEADS, MLA_V).transpose(0, 2, 1, 3)
    k_rope = rope(kr, cos, sin)
    o_b = mla_attention(q_nope, q_rope, k_nope, k_rope, v)
    g_a, g_b = jnp.split(g_merge, 2, axis=-1)
    y = jax.nn.sigmoid(g_a) * (o_a @ w_branch_a) + jax.nn.sigmoid(g_b) * (o_b @ w_branch_b)
    return y @ w_out


def setup_inputs(seed: int = 0) -> dict:
    key = jax.random.key(seed)
    keys = iter(jax.random.split(key, 40))
    nrm = lambda shape, s: jax.random.normal(next(keys), shape, jnp.float32) * s
    L, D = DEPTH, D_MODEL
    return {
        "x": nrm((BATCH, SEQ, D), 1.0),
        "c": nrm((BATCH, D), 1.0),
        "w_ada": nrm((L, D, N_MOD * D), D ** -0.5),
        "b_ada": nrm((L, N_MOD * D), 0.02),
        "ffn1_gate": nrm((L, D, D_FF), D ** -0.5),
        "ffn1_up": nrm((L, D, D_FF), D ** -0.5),
        "ffn1_down": nrm((L, D_FF, D), D_FF ** -0.5),
        "ffn2_gate": nrm((L, D, D_FF), D ** -0.5),
        "ffn2_up": nrm((L, D, D_FF), D ** -0.5),
        "ffn2_down": nrm((L, D_FF, D), D_FF ** -0.5),
        "w_in": nrm((L, D, IN_TOTAL), D ** -0.5),
        "cmpk_w1": nrm((L, CMP_BLOCK * NSA_DK, NSA_DK), (CMP_BLOCK * NSA_DK) ** -0.5),
        "cmpk_w2": nrm((L, NSA_DK, NSA_DK), NSA_DK ** -0.5),
        "cmpk_pe": nrm((L, CMP_BLOCK, NSA_DK), 0.5),
        "cmpv_w1": nrm((L, CMP_BLOCK * NSA_DV, NSA_DV), (CMP_BLOCK * NSA_DV) ** -0.5),
        "cmpv_w2": nrm((L, NSA_DV, NSA_DV), NSA_DV ** -0.5),
        "cmpv_pe": nrm((L, CMP_BLOCK, NSA_DV), 0.5),
        "mla_q_norm": 1.0 + nrm((L, MLA_Q_RANK), 0.02),
        "mla_w_uq": nrm((L, MLA_Q_RANK, MLA_HEADS * (MLA_NOPE + MLA_ROPE)), MLA_Q_RANK ** -0.5),
        "mla_kv_norm": 1.0 + nrm((L, MLA_KV_RANK), 0.02),
        "mla_w_uk": nrm((L, MLA_KV_RANK, MLA_HEADS * MLA_NOPE), MLA_KV_RANK ** -0.5),
        "mla_w_uv": nrm((L, MLA_KV_RANK, MLA_HEADS * MLA_V), MLA_KV_RANK ** -0.5),
        "w_branch_a": nrm((L, NSA_HEADS * NSA_DV, D), (NSA_HEADS * NSA_DV) ** -0.5),
        "w_branch_b": nrm((L, MLA_HEADS * MLA_V, D), (MLA_HEADS * MLA_V) ** -0.5),
        "w_out": nrm((L, D, D), D ** -0.5),
        "final_norm": 1.0 + nrm((D,), 0.02),
    }


def reference(x, c, w_ada, b_ada, ffn1_gate, ffn1_up, ffn1_down, ffn2_gate, ffn2_up, ffn2_down,
              w_in, cmpk_w1, cmpk_w2, cmpk_pe, cmpv_w1, cmpv_w2, cmpv_pe,
              mla_q_norm, mla_w_uq, mla_kv_norm, mla_w_uk, mla_w_uv,
              w_branch_a, w_branch_b, w_out, final_norm):
    S = x.shape[1]
    half = MLA_ROPE // 2
    inv_freq = ROPE_THETA ** (-jnp.arange(half, dtype=jnp.float32) / half)
    ang = jnp.arange(S, dtype=jnp.float32)[:, None] * inv_freq[None, :]
    cos, sin = jnp.cos(ang).astype(x.dtype), jnp.sin(ang).astype(x.dtype)
    c_act = jax.nn.silu(c)
    for l in range(DEPTH):
        mod = c_act @ w_ada[l] + b_ada[l]
        sh1, sc1, g1, sh2, sc2, g2, sh3, sc3, g3 = jnp.split(mod, N_MOD, axis=-1)
        x = x + 0.5 * g1[:, None, :] * swiglu(modulate(x, sh1, sc1), ffn1_gate[l], ffn1_up[l], ffn1_down[l])
        x = x + g2[:, None, :] * token_mix(
            modulate(x, sh2, sc2), cos, sin, w_in[l],
            cmpk_w1[l], cmpk_w2[l], cmpk_pe[l], cmpv_w1[l], cmpv_w2[l], cmpv_pe[l],
            mla_q_norm[l], mla_w_uq[l], mla_kv_norm[l], mla_w_uk[l], mla_w_uv[l],
            w_branch_a[l], w_branch_b[l], w_out[l])
        x = x + 0.5 * g3[:, None, :] * swiglu(modulate(x, sh3, sc3), ffn2_gate[l], ffn2_up[l], ffn2_down[l])
    return rms_norm(x) * final_norm
```

```python
import functools
import math

import jax
import jax.numpy as jnp
from jax import lax
from jax.experimental import pallas as pl
from jax.experimental.pallas import tpu as pltpu

F32 = jnp.float32
BF16 = jnp.bfloat16

NSA_HEADS = 8
NSA_GROUPS = 2
NSA_HPG = NSA_HEADS // NSA_GROUPS
NSA_D = 64
CMP_BLOCK = 32
CMP_STRIDE = 16
SLC_BLOCK = 64
N_SELECT = 16
WINDOW = 512
MLA_HEADS = 8
MLA_NOPE = 64
MLA_ROPE = 32
MLA_V = 64
MLA_Q_RANK = 384
MLA_KV_RANK = 256
ROPE_THETA = 10000.0
N_MOD = 9
Q_BLOCK = 128
NORM_EPS = 1e-6

LANES = 128
SLC_SHIFT = SLC_BLOCK.bit_length() - 1
assert 1 << SLC_SHIFT == SLC_BLOCK
SLOPES = [2.0 ** (-8.0 * (k + 1) / NSA_HEADS) for k in range(NSA_HEADS)]
assert all(math.frexp(v)[0] == 0.5 for v in SLOPES)
NEG = -1e30
VMEM_LIMIT = 56 * 1024 * 1024


def _rms(xf):
    return xf * lax.rsqrt(jnp.mean(xf * xf, -1, keepdims=True) + NORM_EPS)


def _sigmoid(x):
    return 1.0 / (1.0 + jnp.exp(-x))


def _dot(a, b):
    return jnp.dot(a, b, preferred_element_type=F32)


def _dot_nt(a, b, precision=None):
    return lax.dot_general(a, b, (((1,), (1,)), ((), ())), precision=precision,
                           preferred_element_type=F32)


def _params(*sem):
    return pltpu.CompilerParams(dimension_semantics=sem, vmem_limit_bytes=VMEM_LIMIT)


def _resident(shape, index_map):
    return pl.BlockSpec(shape, index_map, pipeline_mode=pl.Buffered(1))


def _adaln_kernel(c_ref, w_ref, b_ref, o_ref):
    c = c_ref[...]
    a = c * _sigmoid(c)
    o_ref[0] = jnp.dot(a, w_ref[0], precision=lax.Precision.HIGHEST,
                       preferred_element_type=F32) + b_ref[0]


def _adaln(c, w_ada, b_ada):
    L, D, N = w_ada.shape
    B = c.shape[0]
    rows = 8
    c_pad = jnp.zeros((rows, D), F32).at[:B].set(c)
    tn = N // 8
    out = pl.pallas_call(
        _adaln_kernel,
        grid=(L, N // tn),
        in_specs=[pl.BlockSpec((rows, D), lambda l, j: (0, 0)),
                  pl.BlockSpec((1, D, tn), lambda l, j: (l, 0, j)),
                  pl.BlockSpec((1, 1, tn), lambda l, j: (l, 0, j))],
        out_specs=pl.BlockSpec((1, rows, tn), lambda l, j: (l, 0, j)),
        out_shape=jax.ShapeDtypeStruct((L, rows, N), F32),
        compiler_params=_params("parallel", "parallel"),
        name="adaln_mod",
    )(c_pad, w_ada, b_ada.reshape(L, 1, N))
    return out[:, :B]


def _ffn_kernel(x_ref, sh_ref, sc_ref, g_ref, wg_ref, wu_ref, wd_ref, fn_ref, o_ref, *, tf, final):
    x = x_ref[0]
    u = (_rms(x) * (1.0 + sc_ref[0]) + sh_ref[0]).astype(BF16)
    n_ff = wg_ref.shape[1]
    acc = jnp.zeros(x.shape, F32)
    for f in range(0, n_ff, tf):
        hg = _dot(u, wg_ref[:, f:f + tf])
        hu = _dot(u, wu_ref[:, f:f + tf])
        a = (hg * _sigmoid(hg) * hu).astype(BF16)
        acc = acc + _dot(a, wd_ref[f:f + tf, :])
    y = x + (0.5 * g_ref[0]) * acc
    if final:
        y = _rms(y) * fn_ref[...]
    o_ref[0] = y


def _ffn(x, sh, sc, g, wg, wu, wd, final_norm, *, final, tm=512):
    B, S, D = x.shape
    n_ff = wg.shape[1]
    tf = n_ff // 2 if (n_ff // 2) % LANES == 0 else n_ff
    vec = pl.BlockSpec((1, 1, D), lambda b, i: (b, 0, 0))
    return pl.pallas_call(
        functools.partial(_ffn_kernel, tf=tf, final=final),
        grid=(B, S // tm),
        in_specs=[pl.BlockSpec((1, tm, D), lambda b, i: (b, i, 0)), vec, vec, vec,
                  _resident((D, n_ff), lambda b, i: (0, 0)),
                  _resident((D, n_ff), lambda b, i: (0, 0)),
                  _resident((n_ff, D), lambda b, i: (0, 0)),
                  pl.BlockSpec((1, D), lambda b, i: (0, 0))],
        out_specs=pl.BlockSpec((1, tm, D), lambda b, i: (b, i, 0)),
        out_shape=jax.ShapeDtypeStruct((B, S, D), F32),
        compiler_params=_params("parallel", "parallel"),
        name="ffn_final" if final else "ffn",
    )(x, sh, sc, g, wg, wu, wd, final_norm.reshape(1, D))


_SEG = {}
_o = 0
for _name, _w in (("qa", 512), ("kcvc", 256), ("kv4", 512), ("gnsa", 256), ("cq", MLA_Q_RANK),
                  ("ckv", MLA_KV_RANK), ("kr", 256), ("gm", 2048)):
    _SEG[_name] = (_o, _o + _w)
    _o += _w
_SEG_TOTAL = _o


def _inproj_kernel(x_ref, sh_ref, sc_ref, w_ref, wqa_ref, wqb_ref, wuk_ref, wuv_ref, qn_ref, kvn_ref,
                   ctab_ref, stab_ref,
                   qa_o, kcvc_o, ks_o, vs_o, kw_o, vw_o, gs_o, qcat_o, kcat_o, vcat_o, gm_o, *, tm):
    i = pl.program_id(1)
    x = x_ref[0]
    u = (_rms(x) * (1.0 + sc_ref[0]) + sh_ref[0]).astype(BF16)

    def proj(name):
        a, b = _SEG[name]
        return _dot(u, w_ref[:, a:b])

    qa_o[0] = proj("qa").astype(BF16)
    kcvc_o[0] = proj("kcvc")

    kv = proj("kv4")
    lane = lax.broadcasted_iota(jnp.int32, (tm, LANES), 1)
    pos = i * tm + lax.broadcasted_iota(jnp.int32, (tm, LANES), 0)
    blk = jnp.right_shift(pos, SLC_SHIFT)
    rem = pos - blk * SLC_BLOCK
    onehot = jnp.where(lane == blk, 1.0, 0.0).astype(BF16)
    zeros = jnp.zeros((tm, LANES), BF16)
    kextra = jnp.where(lane == NSA_D, rem.astype(F32), jnp.where(lane == NSA_D + 1, blk.astype(F32), 0.0))
    vextra = jnp.where(lane == NSA_D, 1.0, 0.0)
    for g in range(NSA_GROUPS):
        for src, k_o, v_o, first in ((0, ks_o, vs_o, onehot), (2, kw_o, vw_o, zeros)):
            kk = kv[:, LANES * src:LANES * (src + 1)]
            vv = kv[:, LANES * (src + 1):LANES * (src + 2)]
            if g == 1:
                kk = pltpu.roll(kk, NSA_D, 1)
                vv = pltpu.roll(vv, NSA_D, 1)
            k_o[0, g, :, 0:LANES] = first
            k_o[0, g, :, LANES:2 * LANES] = jnp.where(lane < NSA_D, kk, kextra).astype(BF16)
            v_o[0, g] = jnp.where(lane < NSA_D, vv, vextra).astype(BF16)

    gs = _sigmoid(proj("gnsa"))
    for g in range(NSA_GROUPS):
        gs_o[0, g] = gs[:, LANES * g:LANES * (g + 1)]

    ctab = ctab_ref[...]
    stab = stab_ref[...]
    scale = (MLA_NOPE + MLA_ROPE) ** -0.5
    cqn = (_rms(proj("cq")) * qn_ref[...]).astype(BF16)
    q_a = _dot(cqn, wqa_ref[...])
    q_b = _dot(cqn, wqb_ref[...])
    ckvn = (_rms(proj("ckv")) * kvn_ref[...]).astype(BF16)
    kn = _dot(ckvn, wuk_ref[...])
    vv = _dot(ckvn, wuv_ref[...])
    kr = proj("kr")
    krope = kr[:, :LANES] * ctab + kr[:, LANES:] * stab
    for h in range(MLA_HEADS):
        sl = slice(LANES * h, LANES * (h + 1))
        qcat_o[0, :, sl] = ((q_a[:, sl] * ctab + q_b[:, sl] * stab) * scale).astype(BF16)
        kcat_o[0, :, sl] = (kn[:, sl] + krope).astype(BF16)
        vcat_o[0, :, sl] = jnp.where(lane == MLA_V, 1.0, vv[:, sl]).astype(BF16)

    gm_o[0] = _sigmoid(proj("gm"))


def _inproj(x, sh, sc, w_all, wqa, wqb, wuk, wuv, qn, kvn, ctab, stab, *, tm=512):
    B, S, D = x.shape
    G = NSA_GROUPS
    grid = (B, S // tm)
    vec = pl.BlockSpec((1, 1, D), lambda b, i: (b, 0, 0))

    def res(a):
        return _resident(a.shape, lambda b, i: (0,) * a.ndim)

    row = lambda w: pl.BlockSpec((1, tm, w), lambda b, i: (b, i, 0))
    grp = lambda w: pl.BlockSpec((1, G, tm, w), lambda b, i: (b, 0, i, 0))
    tab = pl.BlockSpec((tm, LANES), lambda b, i: (i, 0))
    out_shape = (
        jax.ShapeDtypeStruct((B, S, 512), BF16),
        jax.ShapeDtypeStruct((B, S, 256), F32),
        jax.ShapeDtypeStruct((B, G, S, 256), BF16),
        jax.ShapeDtypeStruct((B, G, S, 128), BF16),
        jax.ShapeDtypeStruct((B, G, S, 256), BF16),
        jax.ShapeDtypeStruct((B, G, S, 128), BF16),
        jax.ShapeDtypeStruct((B, G, S, 128), F32),
        jax.ShapeDtypeStruct((B, S, MLA_HEADS * LANES), BF16),
        jax.ShapeDtypeStruct((B, S, MLA_HEADS * LANES), BF16),
        jax.ShapeDtypeStruct((B, S, MLA_HEADS * LANES), BF16),
        jax.ShapeDtypeStruct((B, S, 2048), F32),
    )
    out_specs = (row(512), row(256), grp(256), grp(128), grp(256), grp(128), grp(128),
                 row(MLA_HEADS * LANES), row(MLA_HEADS * LANES), row(MLA_HEADS * LANES), row(2048))
    return pl.pallas_call(
        functools.partial(_inproj_kernel, tm=tm),
        grid=grid,
        in_specs=[pl.BlockSpec((1, tm, D), lambda b, i: (b, i, 0)), vec, vec,
                  res(w_all), res(wqa), res(wqb), res(wuk), res(wuv), res(qn), res(kvn), tab, tab],
        out_specs=out_specs,
        out_shape=out_shape,
        compiler_params=_params("parallel", "parallel"),
        name="in_proj",
    )(x, sh, sc, w_all, wqa, wqb, wuk, wuv, qn, kvn, ctab, stab)


def _compress_kernel(a_ref, pea_ref, peb_ref, wa_ref, wb_ref, w2_ref, kc_o, vct_o):
    a = a_ref[0]
    n = a.shape[0]
    pa = _dot((a + pea_ref[...]).astype(BF16), wa_ref[...])
    pb = _dot((a + peb_ref[...]).astype(BF16), wb_ref[...])
    h = pa + pltpu.roll(pb, n - 1, 0)
    h = 0.5 * h * (1.0 + jnp.tanh(0.7978845608028654 * (h + 0.044715 * (h * h * h))))
    out = _dot(h.astype(BF16), w2_ref[...])
    out_t = out.T
    for g in range(NSA_GROUPS):
        kc_o[0, g] = out[:, NSA_D * g:NSA_D * (g + 1)]
        vct_o[0, g] = out_t[2 * NSA_D + NSA_D * g:2 * NSA_D + NSA_D * (g + 1), :]


def _compress(kcvc, pea, peb, wa, wb, w2):
    B, S, W = kcvc.shape
    n = S // CMP_STRIDE
    a = kcvc.reshape(B, n, CMP_STRIDE * W)
    G = NSA_GROUPS
    full = lambda arr: _resident(arr.shape, lambda b: (0,) * arr.ndim)
    return pl.pallas_call(
        _compress_kernel,
        grid=(B,),
        in_specs=[pl.BlockSpec((1, n, CMP_STRIDE * W), lambda b: (b, 0, 0)),
                  full(pea), full(peb), full(wa), full(wb), full(w2)],
        out_specs=(pl.BlockSpec((1, G, n, NSA_D), lambda b: (b, 0, 0, 0)),
                   pl.BlockSpec((1, G, NSA_D, n), lambda b: (b, 0, 0, 0))),
        out_shape=(jax.ShapeDtypeStruct((B, G, n, NSA_D), F32),
                   jax.ShapeDtypeStruct((B, G, NSA_D, n), F32)),
        compiler_params=_params("parallel"),
        name="compress",
    )(a, pea, peb, wa, wb, w2)


def _nsa_select_kernel(q_ref, kc_ref, vct_ref, oc_o, qaug_o, *, n_top):
    g = pl.program_id(1)
    i = pl.program_id(2)
    tq = q_ref.shape[1]
    ncp = kc_ref.shape[2]
    q0 = i * tq
    kc = kc_ref[0, 0]
    vct = vct_ref[0, 0].astype(BF16)
    scale = NSA_D ** -0.5

    cmp_end = lax.broadcasted_iota(jnp.int32, (ncp, tq), 0) * CMP_STRIDE + (CMP_BLOCK - 1)
    t_c = q0 + lax.broadcasted_iota(jnp.int32, (ncp, tq), 1)
    mask_c = cmp_end <= t_c
    dist_c = (t_c - cmp_end).astype(F32)

    lane = lax.broadcasted_iota(jnp.int32, (tq, LANES), 1)
    qf = q_ref[0].astype(F32)
    psum = jnp.zeros((ncp, tq), F32)
    oc_t = []
    for h in range(NSA_HPG):
        slope = jnp.where(g == 0, SLOPES[h], SLOPES[NSA_HPG + h])
        qh = qf[:, NSA_D * h:NSA_D * (h + 1)]
        s = _dot_nt(kc, qh, precision=lax.Precision.HIGHEST) * scale - slope * dist_c
        s = jnp.where(mask_c, s, -jnp.inf)
        m = jnp.max(s, axis=0, keepdims=True)
        m = jnp.where(m == -jnp.inf, 0.0, m)
        p = jnp.exp(s - m)
        p = p / jnp.maximum(jnp.sum(p, axis=0, keepdims=True), 1.0)
        oc_t.append(_dot(vct, p.astype(BF16)))
        psum = psum + p

        x = qf[:, LANES * (h // 2):LANES * (h // 2 + 1)]
        if h % 2 == 1:
            x = pltpu.roll(x, NSA_D, 1)
        extra = jnp.where(lane == NSA_D, slope, jnp.where(lane == NSA_D + 1, slope * SLC_BLOCK, 0.0))
        qaug_o[0, 0, 0, tq * h:tq * (h + 1), LANES:2 * LANES] = (
            jnp.where(lane < NSA_D, x * scale, extra).astype(BF16))

    oc_o[0] = jnp.concatenate(oc_t, axis=0).T

    ratio = SLC_BLOCK // CMP_STRIDE
    r = CMP_BLOCK // CMP_STRIDE
    bb = lax.broadcasted_iota(jnp.int32, (LANES, ncp), 0)
    mm = lax.broadcasted_iota(jnp.int32, (LANES, ncp), 1)
    a_t = jnp.where((mm >= ratio * bb - (r - 1)) & (mm <= ratio * bb + ratio - 1), 1.0, 0.0).astype(BF16)
    p_hi = psum.astype(BF16)
    p_lo = (psum - p_hi.astype(F32)).astype(BF16)
    imp = _dot(a_t, p_hi) + _dot(a_t, p_lo)

    blk = lax.broadcasted_iota(jnp.int32, (LANES, tq), 0)
    t_s = q0 + lax.broadcasted_iota(jnp.int32, (LANES, tq), 1)
    cur = jnp.right_shift(t_s, SLC_SHIFT)
    forced = (blk == 0) | (blk == cur) | (blk == cur - 1)
    valid = blk * SLC_BLOCK <= t_s
    cand = jnp.where(forced, jnp.inf, jnp.where(valid, imp, -jnp.inf))
    blkf = blk.astype(F32)

    def pick_one(_, carry):
        cand, sel = carry
        m = jnp.max(cand, axis=0, keepdims=True)
        idx = jnp.min(jnp.where(cand == m, blkf, float(LANES)), axis=0, keepdims=True)
        pick = blkf == idx
        return jnp.where(pick, -jnp.inf, cand), jnp.where(pick, 1.0, sel)

    _, sel = lax.fori_loop(0, n_top, pick_one, (cand, jnp.zeros((LANES, tq), F32)))
    pen_t = jnp.where(valid, jnp.where(sel > 0.0, 0.0, NEG), NEG)
    pen = pen_t.T.astype(BF16)
    for h in range(NSA_HPG):
        qaug_o[0, 0, 0, tq * h:tq * (h + 1), 0:LANES] = pen


def _nsa_select(qa, kc, vct, *, n_top):
    B, S, _ = qa.shape
    G = NSA_GROUPS
    tq = Q_BLOCK
    nq = S // tq
    ncp = kc.shape[2]
    gw = NSA_HPG * NSA_D
    return pl.pallas_call(
        functools.partial(_nsa_select_kernel, n_top=n_top),
        grid=(B, G, nq),
        in_specs=[pl.BlockSpec((1, tq, gw), lambda b, g, i: (b, i, g)),
                  pl.BlockSpec((1, 1, ncp, NSA_D), lambda b, g, i: (b, g, 0, 0)),
                  pl.BlockSpec((1, 1, NSA_D, ncp), lambda b, g, i: (b, g, 0, 0))],
        out_specs=(pl.BlockSpec((1, tq, gw), lambda b, g, i: (b, i, g)),
                   pl.BlockSpec((1, 1, 1, NSA_HPG * tq, 2 * LANES), lambda b, g, i: (b, g, i, 0, 0))),
        out_shape=(jax.ShapeDtypeStruct((B, S, G * gw), F32),
                   jax.ShapeDtypeStruct((B, G, nq, NSA_HPG * tq, 2 * LANES), BF16)),
        compiler_params=_params("parallel", "parallel", "parallel"),
        name="nsa_select",
    )(qa, kc, vct)


def _flash_step(q, k, v, m, acc, mask=None):
    s = _dot_nt(q, k)
    if mask is not None:
        s = jnp.where(mask, s, NEG)
    m_new = jnp.maximum(m, jnp.max(s, axis=-1, keepdims=True))
    alpha = jnp.exp(m - m_new)
    p = jnp.exp(s - m_new)
    acc = alpha * acc + _dot(p.astype(BF16), v)
    return m_new, acc


def _nsa_attend_kernel(qaug_ref, ks_ref, vs_ref, kw_ref, vw_ref, oc_ref, gs_ref, o_ref, *, tk):
    i = pl.program_id(2)
    tq = oc_ref.shape[1]
    rows = NSA_HPG * tq
    q = qaug_ref[0, 0, 0]
    q0 = i * tq
    t_row = q0 + (lax.broadcasted_iota(jnp.int32, (rows, 1), 0) & (tq - 1))
    m0 = jnp.full((rows, 1), NEG, F32)
    acc0 = jnp.zeros((rows, LANES), F32)

    n_full = lax.div(q0, tk)

    def sel_body(j, carry):
        off = pl.multiple_of(j * tk, tk)
        return _flash_step(q, ks_ref[0, 0, pl.ds(off, tk), :], vs_ref[0, 0, pl.ds(off, tk), :], *carry)

    m_s, acc_s = lax.fori_loop(0, n_full, sel_body, (m0, acc0))
    off = pl.multiple_of(n_full * tk, tk)
    pos = off + lax.broadcasted_iota(jnp.int32, (rows, tk), 1)
    m_s, acc_s = _flash_step(q, ks_ref[0, 0, pl.ds(off, tk), :], vs_ref[0, 0, pl.ds(off, tk), :],
                             m_s, acc_s, mask=pos <= t_row)

    m_w, acc_w = m0, acc0
    n_w = WINDOW // tq + 1
    for w in range(n_w):
        jt = i - (n_w - 1) + w
        offw = pl.multiple_of(jnp.maximum(jt, 0) * tq, tq)
        posw = jt * tq + lax.broadcasted_iota(jnp.int32, (rows, tq), 1)
        d = t_row - posw
        mask = (d >= 0) & (d < WINDOW) & (posw >= 0)
        m_w, acc_w = _flash_step(q, kw_ref[0, 0, pl.ds(offw, tq), :], vw_ref[0, 0, pl.ds(offw, tq), :],
                                 m_w, acc_w, mask=mask)

    gs = gs_ref[0, 0]
    for h in range(NSA_HPG):
        rs = slice(tq * h, tq * (h + 1))
        o_s = acc_s[rs, 0:NSA_D] / acc_s[rs, NSA_D:NSA_D + 1]
        o_w = acc_w[rs, 0:NSA_D] / acc_w[rs, NSA_D:NSA_D + 1]
        o_c = oc_ref[0, :, NSA_D * h:NSA_D * (h + 1)]
        o = (gs[:, 3 * h:3 * h + 1] * o_c + gs[:, 3 * h + 1:3 * h + 2] * o_s
             + gs[:, 3 * h + 2:3 * h + 3] * o_w)
        o_ref[0, :, NSA_D * h:NSA_D * (h + 1)] = o.astype(BF16)


def _nsa_attend(qaug, ks, vs, kw, vw, oc, gs, *, tk=256):
    B, G, nq = qaug.shape[:3]
    S = ks.shape[2]
    tq = Q_BLOCK
    tk = min(tk, S)
    gw = NSA_HPG * NSA_D
    kv = lambda w: pl.BlockSpec((1, 1, S, w), lambda b, g, i: (b, g, 0, 0))
    return pl.pallas_call(
        functools.partial(_nsa_attend_kernel, tk=tk),
        grid=(B, G, nq),
        in_specs=[pl.BlockSpec((1, 1, 1, NSA_HPG * tq, 2 * LANES), lambda b, g, i: (b, g, i, 0, 0)),
                  kv(2 * LANES), kv(LANES), kv(2 * LANES), kv(LANES),
                  pl.BlockSpec((1, tq, gw), lambda b, g, i: (b, i, g)),
                  pl.BlockSpec((1, 1, tq, LANES), lambda b, g, i: (b, g, i, 0))],
        out_specs=pl.BlockSpec((1, tq, gw), lambda b, g, i: (b, i, g)),
        out_shape=jax.ShapeDtypeStruct((B, S, G * gw), BF16),
        compiler_params=_params("parallel", "parallel", "arbitrary"),
        name="nsa_attend",
    )(qaug, ks, vs, kw, vw, oc, gs)


def _mla_kernel(q_ref, k_ref, v_ref, o_ref, *, t):
    i = pl.program_id(2)
    q = q_ref[0]
    m0 = jnp.full((t, 1), NEG, F32)
    acc0 = jnp.zeros((t, LANES), F32)

    def body(j, carry):
        off = pl.multiple_of(j * t, t)
        return _flash_step(q, k_ref[0, pl.ds(off, t), :], v_ref[0, pl.ds(off, t), :], *carry)

    m, acc = lax.fori_loop(0, i, body, (m0, acc0))
    off = pl.multiple_of(i * t, t)
    causal = (lax.broadcasted_iota(jnp.int32, (t, t), 1) <= lax.broadcasted_iota(jnp.int32, (t, t), 0))
    m, acc = _flash_step(q, k_ref[0, pl.ds(off, t), :], v_ref[0, pl.ds(off, t), :], m, acc, mask=causal)
    o_ref[0] = (acc / acc[:, MLA_V:MLA_V + 1]).astype(BF16)


def _mla_attend(qcat, kcat, vcat, *, t=512):
    B, S, _ = qcat.shape
    t = min(t, S)
    H = MLA_HEADS
    kv = pl.BlockSpec((1, S, LANES), lambda b, h, i: (b, 0, h))
    return pl.pallas_call(
        functools.partial(_mla_kernel, t=t),
        grid=(B, H, S // t),
        in_specs=[pl.BlockSpec((1, t, LANES), lambda b, h, i: (b, i, h)), kv, kv],
        out_specs=pl.BlockSpec((1, t, LANES), lambda b, h, i: (b, i, h)),
        out_shape=jax.ShapeDtypeStruct((B, S, H * LANES), BF16),
        compiler_params=_params("parallel", "parallel", "arbitrary"),
        name="mla_attend",
    )(qcat, kcat, vcat)


def _merge_kernel(x_ref, g_ref, gm_ref, oa_ref, ob_ref, wa_ref, wb_ref, wo_ref, o_ref):
    D = x_ref.shape[2]
    gm = gm_ref[0]
    y = gm[:, :D] * _dot(oa_ref[0], wa_ref[...]) + gm[:, D:] * _dot(ob_ref[0], wb_ref[...])
    o_ref[0] = x_ref[0] + g_ref[0] * _dot(y.astype(BF16), wo_ref[...])


def _merge(x, g, gm, oa, ob, wa, wb, wo, *, tm=512):
    B, S, D = x.shape
    row = lambda w: pl.BlockSpec((1, tm, w), lambda b, i: (b, i, 0))
    res = lambda a: _resident(a.shape, lambda b, i: (0, 0))
    return pl.pallas_call(
        _merge_kernel,
        grid=(B, S // tm),
        in_specs=[row(D), pl.BlockSpec((1, 1, D), lambda b, i: (b, 0, 0)), row(2 * D),
                  row(oa.shape[2]), row(ob.shape[2]), res(wa), res(wb), res(wo)],
        out_specs=row(D),
        out_shape=jax.ShapeDtypeStruct((B, S, D), F32),
        compiler_params=_params("parallel", "parallel"),
        name="merge",
    )(x, g, gm, oa, ob, wa, wb, wo)


def _pad_heads(w, n_heads, width):
    k = w.shape[0]
    w = w.reshape(k, n_heads, width)
    return jnp.pad(w, ((0, 0), (0, 0), (0, LANES - width))).reshape(k, n_heads * LANES)


def _rot_cols(w):
    half = w.shape[-1] // 2
    return jnp.concatenate([-w[..., half:], w[..., :half]], axis=-1)


def _prep_inproj_weights(w_in, w_uq, w_uk, w_uv):
    D = w_in.shape[0]
    o = 0
    seg = {}
    for name, width in (("qa", 512), ("kc", 128), ("vc", 128), ("ks", 128), ("vs", 128), ("kw", 128),
                        ("vw", 128), ("gnsa", 3 * NSA_HEADS), ("cq", MLA_Q_RANK), ("ckv", MLA_KV_RANK),
                        ("kr", MLA_ROPE), ("gm", 2048)):
        seg[name] = w_in[:, o:o + width]
        o += width
    per_g = 3 * NSA_HPG
    gn = jnp.concatenate(
        [jnp.pad(seg["gnsa"][:, per_g * g:per_g * (g + 1)], ((0, 0), (0, LANES - per_g)))
         for g in range(NSA_GROUPS)], axis=1)
    z64 = jnp.zeros((D, MLA_NOPE), F32)
    z32 = jnp.zeros((D, LANES - MLA_NOPE - MLA_ROPE), F32)
    kr = jnp.concatenate([z64, seg["kr"], z32, z64, _rot_cols(seg["kr"]), z32], axis=1)
    w_all = jnp.concatenate([seg["qa"], seg["kc"], seg["vc"], seg["ks"], seg["vs"], seg["kw"], seg["vw"],
                             gn, seg["cq"], seg["ckv"], kr, seg["gm"]], axis=1).astype(BF16)
    assert w_all.shape[1] == _SEG_TOTAL
    hd = MLA_NOPE + MLA_ROPE
    uq = w_uq.reshape(-1, MLA_HEADS, hd)
    uq_rot = jnp.concatenate([jnp.zeros_like(uq[..., :MLA_NOPE]), _rot_cols(uq[..., MLA_NOPE:])], axis=-1)
    wqa = _pad_heads(w_uq, MLA_HEADS, hd).astype(BF16)
    wqb = _pad_heads(uq_rot.reshape(-1, MLA_HEADS * hd), MLA_HEADS, hd).astype(BF16)
    wuk = _pad_heads(w_uk, MLA_HEADS, MLA_NOPE).astype(BF16)
    wuv = _pad_heads(w_uv, MLA_HEADS, MLA_V).astype(BF16)
    return w_all, wqa, wqb, wuk, wuv


def _prep_compress_weights(k_w1, k_w2, k_pe, v_w1, v_w2, v_pe):
    d = NSA_D
    half = CMP_STRIDE

    def blockdiag(mats):
        n = len(mats)
        z = jnp.zeros_like(mats[0])
        rows = [jnp.concatenate([mats[r] if c == r else z for c in range(n)], axis=-1) for r in range(n)]
        return jnp.concatenate(rows, axis=-2)

    def first(w1, lo):
        return w1.reshape(CMP_BLOCK, d, d)[lo:lo + half]

    def w1_half(lo):
        k, v = first(k_w1, lo), first(v_w1, lo)
        return blockdiag([k, k, v, v]).reshape(half * 4 * d, 4 * d).astype(BF16)

    def pe_half(lo):
        k, v = k_pe[lo:lo + half], v_pe[lo:lo + half]
        return jnp.concatenate([k, k, v, v], axis=-1).reshape(1, half * 4 * d)

    w2 = blockdiag([k_w2, k_w2, v_w2, v_w2]).astype(BF16)
    return pe_half(0), pe_half(half), w1_half(0), w1_half(half), w2


def _rope_tables(S):
    half = MLA_ROPE // 2
    inv_freq = ROPE_THETA ** (-jnp.arange(half, dtype=F32) / half)
    ang = jnp.arange(S, dtype=F32)[:, None] * inv_freq[None, :]
    cos, sin = jnp.cos(ang), jnp.sin(ang)
    ones = jnp.ones((S, MLA_NOPE), F32)
    z_n = jnp.zeros((S, MLA_NOPE), F32)
    z_p = jnp.zeros((S, LANES - MLA_NOPE - MLA_ROPE), F32)
    ctab = jnp.concatenate([ones, cos, cos, z_p], axis=1)
    stab = jnp.concatenate([z_n, sin, sin, z_p], axis=1)
    return ctab, stab


def kernel(x, c, w_ada, b_ada, ffn1_gate, ffn1_up, ffn1_down, ffn2_gate, ffn2_up, ffn2_down, w_in, cmpk_w1, cmpk_w2, cmpk_pe, cmpv_w1, cmpv_w2, cmpv_pe, mla_q_norm, mla_w_uq, mla_kv_norm, mla_w_uk, mla_w_uv, w_branch_a, w_branch_b, w_out, final_norm):
    B, S, D = x.shape
    L = w_ada.shape[0]
    assert NSA_GROUPS == 2 and S // SLC_BLOCK <= LANES and S % 512 == 0
    n_top = min(N_SELECT, S // SLC_BLOCK)
    ctab, stab = _rope_tables(S)
    mod = _adaln(c, w_ada, b_ada).reshape(L, B, N_MOD, 1, D)
    bf = lambda w: w.astype(BF16)
    for l in range(L):
        sh1, sc1, g1, sh2, sc2, g2, sh3, sc3, g3 = (mod[l, :, k] for k in range(N_MOD))
        x = _ffn(x, sh1, sc1, g1, bf(ffn1_gate[l]), bf(ffn1_up[l]), bf(ffn1_down[l]), final_norm, final=False)

        w_all, wqa, wqb, wuk, wuv = _prep_inproj_weights(w_in[l], mla_w_uq[l], mla_w_uk[l], mla_w_uv[l])
        (qa, kcvc, ks, vs, kw, vw, gs, qcat, kcat, vcat, gm) = _inproj(
            x, sh2, sc2, w_all, wqa, wqb, wuk, wuv,
            mla_q_norm[l].reshape(1, -1), mla_kv_norm[l].reshape(1, -1), ctab, stab)
        kc, vct = _compress(kcvc, *_prep_compress_weights(
            cmpk_w1[l], cmpk_w2[l], cmpk_pe[l], cmpv_w1[l], cmpv_w2[l], cmpv_pe[l]))
        oc, qaug = _nsa_select(qa, kc, vct, n_top=n_top)
        oa = _nsa_attend(qaug, ks, vs, kw, vw, oc, gs)
        ob = _mla_attend(qcat, kcat, vcat)
        wb_pad = jnp.pad(w_branch_b[l].reshape(MLA_HEADS, MLA_V, D),
                         ((0, 0), (0, LANES - MLA_V), (0, 0))).reshape(MLA_HEADS * LANES, D)
        x = _merge(x, g2, gm, oa, ob, bf(w_branch_a[l]), bf(wb_pad), bf(w_out[l]))

        x = _ffn(x, sh3, sc3, g3, bf(ffn2_gate[l]), bf(ffn2_up[l]), bf(ffn2_down[l]), final_norm,
                 final=(l == L - 1))
    return x
```

```python
import functools
import math

import ml_dtypes
import numpy as np
import jax
import jax.numpy as jnp
from jax import lax
from jax.experimental import pallas as pl
from jax.experimental.pallas import tpu as pltpu

F32 = jnp.float32
BF16 = jnp.bfloat16

NSA_HEADS = 8
NSA_GROUPS = 2
NSA_HPG = NSA_HEADS // NSA_GROUPS
NSA_D = 64
CMP_BLOCK = 32
CMP_STRIDE = 16
SLC_BLOCK = 64
N_SELECT = 16
WINDOW = 512
MLA_HEADS = 8
MLA_NOPE = 64
MLA_ROPE = 32
MLA_V = 64
MLA_Q_RANK = 384
MLA_KV_RANK = 256
ROPE_THETA = 10000.0
N_MOD = 9
Q_BLOCK = 128
NORM_EPS = 1e-6

LANES = 128
SLC_SHIFT = SLC_BLOCK.bit_length() - 1
assert 1 << SLC_SHIFT == SLC_BLOCK
LOG2E = math.log2(math.e)
SLOPES = [2.0 ** (-8.0 * (k + 1) / NSA_HEADS) for k in range(NSA_HEADS)]
N_PIECES = 3


def _bf16_pieces(v):
    out = []
    for _ in range(N_PIECES):
        p = float(np.float32(v).astype(ml_dtypes.bfloat16))
        out.append(p)
        v -= p
    return out


SLOPE_PIECES = [_bf16_pieces(v * LOG2E) for v in SLOPES]
NEG = -1e30
VMEM_LIMIT = 56 * 1024 * 1024


def _rms(xf):
    return xf * lax.rsqrt(jnp.mean(xf * xf, -1, keepdims=True) + NORM_EPS)


def _sigmoid(x):
    return 1.0 / (1.0 + jnp.exp(-x))


def _dot(a, b):
    return jnp.dot(a, b, preferred_element_type=F32)


def _dot_nt(a, b, precision=None):
    return lax.dot_general(a, b, (((1,), (1,)), ((), ())), precision=precision,
                           preferred_element_type=F32)


def _params(*sem):
    return pltpu.CompilerParams(dimension_semantics=sem, vmem_limit_bytes=VMEM_LIMIT)


def _resident(shape, index_map):
    return pl.BlockSpec(shape, index_map, pipeline_mode=pl.Buffered(1))


def _adaln_kernel(c_ref, w_ref, b_ref, o_ref):
    c = c_ref[...]
    a = c * _sigmoid(c)
    o_ref[0] = jnp.dot(a, w_ref[0], precision=lax.Precision.HIGHEST,
                       preferred_element_type=F32) + b_ref[0]


def _adaln(c, w_ada, b_ada):
    L, D, N = w_ada.shape
    B = c.shape[0]
    rows = 8
    c_pad = jnp.zeros((rows, D), F32).at[:B].set(c)
    tn = N // 8
    out = pl.pallas_call(
        _adaln_kernel,
        grid=(L, N // tn),
        in_specs=[pl.BlockSpec((rows, D), lambda l, j: (0, 0)),
                  pl.BlockSpec((1, D, tn), lambda l, j: (l, 0, j)),
                  pl.BlockSpec((1, 1, tn), lambda l, j: (l, 0, j))],
        out_specs=pl.BlockSpec((1, rows, tn), lambda l, j: (l, 0, j)),
        out_shape=jax.ShapeDtypeStruct((L, rows, N), F32),
        compiler_params=_params("parallel", "parallel"),
        name="adaln_mod",
    )(c_pad, w_ada, b_ada.reshape(L, 1, N))
    return out[:, :B]


def _ffn_kernel(x_ref, sh_ref, sc_ref, g_ref, wg_ref, wu_ref, wd_ref, fn_ref, o_ref, *, tf, final):
    x = x_ref[0]
    u = (_rms(x) * (1.0 + sc_ref[0]) + sh_ref[0]).astype(BF16)
    n_ff = wg_ref.shape[1]
    acc = jnp.zeros(x.shape, F32)
    for f in range(0, n_ff, tf):
        hg = _dot(u, wg_ref[:, f:f + tf])
        hu = _dot(u, wu_ref[:, f:f + tf])
        a = (hg * _sigmoid(hg) * hu).astype(BF16)
        acc = acc + _dot(a, wd_ref[f:f + tf, :])
    y = x + (0.5 * g_ref[0]) * acc
    if final:
        y = _rms(y) * fn_ref[...]
    o_ref[0] = y


def _ffn(x, sh, sc, g, wg, wu, wd, final_norm, *, final, tm=512):
    B, S, D = x.shape
    n_ff = wg.shape[1]
    tf = n_ff // 2 if (n_ff // 2) % LANES == 0 else n_ff
    vec = pl.BlockSpec((1, 1, D), lambda b, i: (b, 0, 0))
    return pl.pallas_call(
        functools.partial(_ffn_kernel, tf=tf, final=final),
        grid=(B, S // tm),
        in_specs=[pl.BlockSpec((1, tm, D), lambda b, i: (b, i, 0)), vec, vec, vec,
                  _resident((D, n_ff), lambda b, i: (0, 0)),
                  _resident((D, n_ff), lambda b, i: (0, 0)),
                  _resident((n_ff, D), lambda b, i: (0, 0)),
                  pl.BlockSpec((1, D), lambda b, i: (0, 0))],
        out_specs=pl.BlockSpec((1, tm, D), lambda b, i: (b, i, 0)),
        out_shape=jax.ShapeDtypeStruct((B, S, D), F32),
        compiler_params=_params("parallel", "parallel"),
        name="ffn_final" if final else "ffn",
    )(x, sh, sc, g, wg, wu, wd, final_norm.reshape(1, D))


_SEG = {}
_o = 0
for _name, _w in (("qa", 512), ("kcvc", 256), ("kv4", 512), ("gnsa", 256), ("cq", MLA_Q_RANK),
                  ("ckv", MLA_KV_RANK), ("kr", 256), ("gm", 2048)):
    _SEG[_name] = (_o, _o + _w)
    _o += _w
_SEG_TOTAL = _o


def _inproj_kernel(x_ref, sh_ref, sc_ref, w_ref, wqa_ref, wqb_ref, wuk_ref, wuv_ref, qn_ref, kvn_ref,
                   ctab_ref, stab_ref,
                   qa_o, kcvc_o, ks_o, vs_o, kw_o, vw_o, gs_o, qcat_o, kcat_o, vcat_o, gm_o, *, tm):
    i = pl.program_id(1)
    x = x_ref[0]
    u = (_rms(x) * (1.0 + sc_ref[0]) + sh_ref[0]).astype(BF16)

    def proj(name):
        a, b = _SEG[name]
        return _dot(u, w_ref[:, a:b])

    qa_o[0] = proj("qa").astype(BF16)
    kcvc_o[0] = proj("kcvc")

    kv = proj("kv4")
    lane = lax.broadcasted_iota(jnp.int32, (tm, LANES), 1)
    pos = i * tm + lax.broadcasted_iota(jnp.int32, (tm, LANES), 0)
    blk = jnp.right_shift(pos, SLC_SHIFT)
    rem = pos - blk * SLC_BLOCK
    onehot = jnp.where(lane == blk, 1.0, 0.0).astype(BF16)
    zeros = jnp.zeros((tm, LANES), BF16)
    kextra = jnp.where(lane < NSA_D + N_PIECES, rem.astype(F32),
                       jnp.where(lane < NSA_D + 2 * N_PIECES, blk.astype(F32), 0.0))
    vextra = jnp.where(lane == NSA_D, 1.0, 0.0)
    for g in range(NSA_GROUPS):
        for src, k_o, v_o, first in ((0, ks_o, vs_o, onehot), (2, kw_o, vw_o, zeros)):
            kk = kv[:, LANES * src:LANES * (src + 1)]
            vv = kv[:, LANES * (src + 1):LANES * (src + 2)]
            if g == 1:
                kk = pltpu.roll(kk, NSA_D, 1)
                vv = pltpu.roll(vv, NSA_D, 1)
            k_o[0, g, :, 0:LANES] = first
            k_o[0, g, :, LANES:2 * LANES] = jnp.where(lane < NSA_D, kk, kextra).astype(BF16)
            v_o[0, g] = jnp.where(lane < NSA_D, vv, vextra).astype(BF16)

    gs = _sigmoid(proj("gnsa"))
    for g in range(NSA_GROUPS):
        gs_o[0, g] = gs[:, LANES * g:LANES * (g + 1)]

    ctab = ctab_ref[...]
    stab = stab_ref[...]
    scale = (MLA_NOPE + MLA_ROPE) ** -0.5 * LOG2E
    cqn =(_rms(proj("cq")) * qn_ref[...]).astype(BF16)
    q_a = _dot(cqn, wqa_ref[...])
    q_b = _dot(cqn, wqb_ref[...])
    ckvn = (_rms(proj("ckv")) * kvn_ref[...]).astype(BF16)
    kn = _dot(ckvn, wuk_ref[...])
    vv = _dot(ckvn, wuv_ref[...])
    kr = proj("kr")
    krope = kr[:, :LANES] * ctab + kr[:, LANES:] * stab
    for h in range(MLA_HEADS):
        sl = slice(LANES * h, LANES * (h + 1))
        qcat_o[0, :, sl] = ((q_a[:, sl] * ctab + q_b[:, sl] * stab) * scale).astype(BF16)
        kcat_o[0, :, sl] = (kn[:, sl] + krope).astype(BF16)
        vcat_o[0, :, sl] = jnp.where(lane == MLA_V, 1.0, vv[:, sl]).astype(BF16)

    gm_o[0] = _sigmoid(proj("gm"))


def _inproj(x, sh, sc, w_all, wqa, wqb, wuk, wuv, qn, kvn, ctab, stab, *, tm=512):
    B, S, D = x.shape
    G = NSA_GROUPS
    grid = (B, S // tm)
    vec = pl.BlockSpec((1, 1, D), lambda b, i: (b, 0, 0))

    def res(a):
        return _resident(a.shape, lambda b, i: (0,) * a.ndim)

    row = lambda w: pl.BlockSpec((1, tm, w), lambda b, i: (b, i, 0))
    grp = lambda w: pl.BlockSpec((1, G, tm, w), lambda b, i: (b, 0, i, 0))
    tab = pl.BlockSpec((tm, LANES), lambda b, i: (i, 0))
    out_shape = (
        jax.ShapeDtypeStruct((B, S, 512), BF16),
        jax.ShapeDtypeStruct((B, S, 256), F32),
        jax.ShapeDtypeStruct((B, G, S, 256), BF16),
        jax.ShapeDtypeStruct((B, G, S, 128), BF16),
        jax.ShapeDtypeStruct((B, G, S, 256), BF16),
        jax.ShapeDtypeStruct((B, G, S, 128), BF16),
        jax.ShapeDtypeStruct((B, G, S, 128), F32),
        jax.ShapeDtypeStruct((B, S, MLA_HEADS * LANES), BF16),
        jax.ShapeDtypeStruct((B, S, MLA_HEADS * LANES), BF16),
        jax.ShapeDtypeStruct((B, S, MLA_HEADS * LANES), BF16),
        jax.ShapeDtypeStruct((B, S, 2048), F32),
    )
    out_specs = (row(512), row(256), grp(256), grp(128), grp(256), grp(128), grp(128),
                 row(MLA_HEADS * LANES), row(MLA_HEADS * LANES), row(MLA_HEADS * LANES), row(2048))
    return pl.pallas_call(
        functools.partial(_inproj_kernel, tm=tm),
        grid=grid,
        in_specs=[pl.BlockSpec((1, tm, D), lambda b, i: (b, i, 0)), vec, vec,
                  res(w_all), res(wqa), res(wqb), res(wuk), res(wuv), res(qn), res(kvn), tab, tab],
        out_specs=out_specs,
        out_shape=out_shape,
        compiler_params=_params("parallel", "parallel"),
        name="in_proj",
    )(x, sh, sc, w_all, wqa, wqb, wuk, wuv, qn, kvn, ctab, stab)


def _compress_kernel(a_ref, pea_ref, peb_ref, wa_ref, wb_ref, w2_ref, kc_o, vct_o):
    a = a_ref[0]
    n = a.shape[0]
    pa = _dot((a + pea_ref[...]).astype(BF16), wa_ref[...])
    pb = _dot((a + peb_ref[...]).astype(BF16), wb_ref[...])
    h = pa + pltpu.roll(pb, n - 1, 0)
    h = 0.5 * h * (1.0 + jnp.tanh(0.7978845608028654 * (h + 0.044715 * (h * h * h))))
    out = _dot(h.astype(BF16), w2_ref[...])
    out_t = out.T
    for g in range(NSA_GROUPS):
        kc_o[0, g] = out[:, NSA_D * g:NSA_D * (g + 1)]
        vct_o[0, g] = out_t[2 * NSA_D + NSA_D * g:2 * NSA_D + NSA_D * (g + 1), :]


def _compress(kcvc, pea, peb, wa, wb, w2):
    B, S, W = kcvc.shape
    n = S // CMP_STRIDE
    a = kcvc.reshape(B, n, CMP_STRIDE * W)
    G = NSA_GROUPS
    full = lambda arr: _resident(arr.shape, lambda b: (0,) * arr.ndim)
    return pl.pallas_call(
        _compress_kernel,
        grid=(B,),
        in_specs=[pl.BlockSpec((1, n, CMP_STRIDE * W), lambda b: (b, 0, 0)),
                  full(pea), full(peb), full(wa), full(wb), full(w2)],
        out_specs=(pl.BlockSpec((1, G, n, NSA_D), lambda b: (b, 0, 0, 0)),
                   pl.BlockSpec((1, G, NSA_D, n), lambda b: (b, 0, 0, 0))),
        out_shape=(jax.ShapeDtypeStruct((B, G, n, NSA_D), F32),
                   jax.ShapeDtypeStruct((B, G, NSA_D, n), F32)),
        compiler_params=_params("parallel"),
        name="compress",
    )(a, pea, peb, wa, wb, w2)


def _nsa_select_kernel(q_ref, kc_ref, vct_ref, oc_o, qaug_o, *, n_top):
    g = pl.program_id(1)
    i = pl.program_id(2)
    tq = q_ref.shape[1]
    ncp = kc_ref.shape[2]
    q0 = i * tq
    kc = kc_ref[0, 0]
    vct = vct_ref[0, 0].astype(BF16)
    scale = NSA_D ** -0.5

    cmp_end = lax.broadcasted_iota(jnp.int32, (ncp, tq), 0) * CMP_STRIDE + (CMP_BLOCK - 1)
    t_c = q0 + lax.broadcasted_iota(jnp.int32, (ncp, tq), 1)
    mask_c = cmp_end <= t_c
    dist_c = (t_c - cmp_end).astype(F32)

    lane = lax.broadcasted_iota(jnp.int32, (tq, LANES), 1)
    qf = q_ref[0].astype(F32)
    psum = jnp.zeros((ncp, tq), F32)
    oc_t = []
    for h in range(NSA_HPG):
        slope = jnp.where(g == 0, SLOPES[h], SLOPES[NSA_HPG + h])
        qh = qf[:, NSA_D * h:NSA_D * (h + 1)]
        s = _dot_nt(kc, qh, precision=lax.Precision.HIGHEST) * scale - slope * dist_c
        s = jnp.where(mask_c, s, -jnp.inf)
        m = jnp.max(s, axis=0, keepdims=True)
        m = jnp.where(m == -jnp.inf, 0.0, m)
        p = jnp.exp(s - m)
        p = p / jnp.maximum(jnp.sum(p, axis=0, keepdims=True), 1.0)
        oc_t.append(_dot(vct, p.astype(BF16)))
        psum = psum + p

        x = qf[:, LANES * (h // 2):LANES * (h // 2 + 1)]
        if h % 2 == 1:
            x = pltpu.roll(x, NSA_D, 1)
        extra = jnp.zeros((tq, LANES), F32)
        for k in range(N_PIECES):
            piece = jnp.where(g == 0, SLOPE_PIECES[h][k], SLOPE_PIECES[NSA_HPG + h][k])
            extra = jnp.where(lane == NSA_D + k, piece,
                              jnp.where(lane == NSA_D + N_PIECES + k, piece * SLC_BLOCK, extra))
        qaug_o[0, 0, 0, tq * h:tq * (h + 1), LANES:2 * LANES] = (
            jnp.where(lane < NSA_D, x * (scale * LOG2E), extra).astype(BF16))

    oc_o[0] = jnp.concatenate(oc_t, axis=0).T

    ratio = SLC_BLOCK // CMP_STRIDE
    r = CMP_BLOCK // CMP_STRIDE
    bb = lax.broadcasted_iota(jnp.int32, (LANES, ncp), 0)
    mm = lax.broadcasted_iota(jnp.int32, (LANES, ncp), 1)
    a_t = jnp.where((mm >= ratio * bb - (r - 1)) & (mm <= ratio * bb + ratio - 1), 1.0, 0.0).astype(BF16)
    p_hi = psum.astype(BF16)
    p_lo = (psum - p_hi.astype(F32)).astype(BF16)
    imp = _dot(a_t, p_hi) + _dot(a_t, p_lo)

    blk = lax.broadcasted_iota(jnp.int32, (LANES, tq), 0)
    t_s = q0 + lax.broadcasted_iota(jnp.int32, (LANES, tq), 1)
    cur = jnp.right_shift(t_s, SLC_SHIFT)
    forced = (blk == 0) | (blk == cur) | (blk == cur - 1)
    valid = blk * SLC_BLOCK <= t_s
    cand = jnp.where(forced, jnp.inf, jnp.where(valid, imp, -jnp.inf))
    blkf = blk.astype(F32)

    def pick_one(_, carry):
        cand, sel = carry
        m = jnp.max(cand, axis=0, keepdims=True)
        idx = jnp.min(jnp.where(cand == m, blkf, float(LANES)), axis=0, keepdims=True)
        pick = blkf == idx
        return jnp.where(pick, -jnp.inf, cand), jnp.where(pick, 1.0, sel)

    _, sel = lax.fori_loop(0, n_top, pick_one, (cand, jnp.zeros((LANES, tq), F32)))
    pen_t = jnp.where(valid, jnp.where(sel > 0.0, 0.0, NEG), NEG)
    pen = pen_t.T.astype(BF16)
    for h in range(NSA_HPG):
        qaug_o[0, 0, 0, tq * h:tq * (h + 1), 0:LANES] = pen


def _nsa_select(qa, kc, vct, *, n_top):
    B, S, _ = qa.shape
    G = NSA_GROUPS
    tq = Q_BLOCK
    nq = S // tq
    ncp = kc.shape[2]
    gw = NSA_HPG * NSA_D
    return pl.pallas_call(
        functools.partial(_nsa_select_kernel, n_top=n_top),
        grid=(B, G, nq),
        in_specs=[pl.BlockSpec((1, tq, gw), lambda b, g, i: (b, i, g)),
                  pl.BlockSpec((1, 1, ncp, NSA_D), lambda b, g, i: (b, g, 0, 0)),
                  pl.BlockSpec((1, 1, NSA_D, ncp), lambda b, g, i: (b, g, 0, 0))],
        out_specs=(pl.BlockSpec((1, tq, gw), lambda b, g, i: (b, i, g)),
                   pl.BlockSpec((1, 1, 1, NSA_HPG * tq, 2 * LANES), lambda b, g, i: (b, g, i, 0, 0))),
        out_shape=(jax.ShapeDtypeStruct((B, S, G * gw), F32),
                   jax.ShapeDtypeStruct((B, G, nq, NSA_HPG * tq, 2 * LANES), BF16)),
        compiler_params=_params("parallel", "parallel", "parallel"),
        name="nsa_select",
    )(qa, kc, vct)


CHUNK = 64
ROW_PARTS = 1


def _flash_tile(q, k, v, s_scr, p_scr, m_scr, acc_scr, bias_fn=None):
    rows, n = q.shape[0], k.shape[0]
    part = rows // ROW_PARTS
    for h0 in range(0, rows, part):
        hs = slice(h0, h0 + part)
        s_scr[hs, :n] = _dot_nt(q[hs], k)
        for r0 in range(h0, h0 + part, CHUNK):
            rs = slice(r0, r0 + CHUNK)
            s = s_scr[rs, :n]
            if bias_fn is not None:
                s = s + bias_fn(r0)
            m_old = m_scr[rs]
            m_new = jnp.maximum(m_old, jnp.broadcast_to(jnp.max(s, axis=-1, keepdims=True), m_old.shape))
            m_scr[rs] = m_new
            acc_scr[rs] = acc_scr[rs] * jnp.exp2(m_old - m_new)
            for c in range(0, n, LANES):
                p_scr[rs, c:c + LANES] = jnp.exp2(s[:, c:c + LANES] - m_new).astype(BF16)
        acc_scr[hs] += _dot(p_scr[hs, :n], v)


def _causal_bias(t0, pos0, n):
    t = t0 + lax.broadcasted_iota(jnp.int32, (CHUNK, n), 0)
    pos = pos0 + lax.broadcasted_iota(jnp.int32, (CHUNK, n), 1)
    return jnp.where(pos <= t, 0.0, NEG)


def _nsa_attend_kernel(qaug_ref, ks_ref, vs_ref, kw_ref, vw_ref, oc_ref, gs_ref, o_ref,
                       s_scr, p_scr, m_scr, accs_scr, accw_scr, *, tk):
    i = pl.program_id(2)
    tq = oc_ref.shape[1]
    rows = NSA_HPG * tq
    q = qaug_ref[0, 0, 0]
    q0 = i * tq
    row_t0 = lambda r0: q0 + r0 % tq

    m_scr[...] = jnp.full((rows, LANES), NEG, F32)
    accs_scr[...] = jnp.zeros((rows, LANES), F32)
    n_full = lax.div(q0, tk)

    def sel_body(j, carry):
        off = pl.multiple_of(j * tk, tk)
        _flash_tile(q, ks_ref[0, 0, pl.ds(off, tk), :], vs_ref[0, 0, pl.ds(off, tk), :],
                    s_scr, p_scr, m_scr, accs_scr)
        return carry

    lax.fori_loop(0, n_full, sel_body, 0)
    off = pl.multiple_of(n_full * tk, tk)
    _flash_tile(q, ks_ref[0, 0, pl.ds(off, tk), :], vs_ref[0, 0, pl.ds(off, tk), :],
                s_scr, p_scr, m_scr, accs_scr, bias_fn=lambda r0: _causal_bias(row_t0(r0), off, tk))

    nw = WINDOW + tq
    offw = pl.multiple_of(jnp.maximum(q0 - WINDOW, 0), tq)

    def window_bias(r0):
        d = (row_t0(r0) + lax.broadcasted_iota(jnp.int32, (CHUNK, nw), 0)
             - (offw + lax.broadcasted_iota(jnp.int32, (CHUNK, nw), 1)))
        return jnp.where(d >= 0, jnp.where(d < WINDOW, 0.0, NEG), NEG)

    m_scr[...] = jnp.full((rows, LANES), NEG, F32)
    accw_scr[...] = jnp.zeros((rows, LANES), F32)
    _flash_tile(q, kw_ref[0, 0, pl.ds(offw, nw), :], vw_ref[0, 0, pl.ds(offw, nw), :],
                s_scr, p_scr, m_scr, accw_scr, bias_fn=window_bias)

    gs = gs_ref[0, 0]
    for h in range(NSA_HPG):
        rs = slice(tq * h, tq * (h + 1))
        acc_s = accs_scr[rs]
        acc_w = accw_scr[rs]
        o_s = acc_s[:, 0:NSA_D] / acc_s[:, NSA_D:NSA_D + 1]
        o_w = acc_w[:, 0:NSA_D] / acc_w[:, NSA_D:NSA_D + 1]
        o_c = oc_ref[0, :, NSA_D * h:NSA_D * (h + 1)]
        o = (gs[:, 3 * h:3 * h + 1] * o_c + gs[:, 3 * h + 1:3 * h + 2] * o_s
             + gs[:, 3 * h + 2:3 * h + 3] * o_w)
        o_ref[0, :, NSA_D * h:NSA_D * (h + 1)] = o.astype(BF16)


def _nsa_attend(qaug, ks, vs, kw, vw, oc, gs, *, tk=512):
    B, G, nq = qaug.shape[:3]
    S = ks.shape[2]
    tq = Q_BLOCK
    tk = min(tk, S)
    gw = NSA_HPG * NSA_D
    rows = NSA_HPG * tq
    wide = max(tk, WINDOW + tq)
    assert S >= WINDOW + tq
    kv = lambda w: pl.BlockSpec((1, 1, S, w), lambda b, g, i: (b, g, 0, 0))
    return pl.pallas_call(
        functools.partial(_nsa_attend_kernel, tk=tk),
        grid=(B, G, nq),
        in_specs=[pl.BlockSpec((1, 1, 1, NSA_HPG * tq, 2 * LANES), lambda b, g, i: (b, g, i, 0, 0)),
                  kv(2 * LANES), kv(LANES), kv(2 * LANES), kv(LANES),
                  pl.BlockSpec((1, tq, gw), lambda b, g, i: (b, i, g)),
                  pl.BlockSpec((1, 1, tq, LANES), lambda b, g, i: (b, g, i, 0))],
        out_specs=pl.BlockSpec((1, tq, gw), lambda b, g, i: (b, i, g)),
        out_shape=jax.ShapeDtypeStruct((B, S, G * gw), BF16),
        scratch_shapes=[pltpu.VMEM((rows, wide), F32), pltpu.VMEM((rows, wide), BF16),
                        pltpu.VMEM((rows, LANES), F32), pltpu.VMEM((rows, LANES), F32),
                        pltpu.VMEM((rows, LANES), F32)],
        compiler_params=_params("parallel", "parallel", "arbitrary"),
        name="nsa_attend",
    )(qaug, ks, vs, kw, vw, oc, gs)


def _mla_kernel(q_ref, k_ref, v_ref, o_ref, s_scr, p_scr, m_scr, acc_scr, *, t):
    i = pl.program_id(2)
    q = q_ref[0]
    m_scr[...] = jnp.full((t, LANES), NEG, F32)
    acc_scr[...] = jnp.zeros((t, LANES), F32)

    def body(j, carry):
        off = pl.multiple_of(j * t, t)
        _flash_tile(q, k_ref[0, pl.ds(off, t), :], v_ref[0, pl.ds(off, t), :], s_scr, p_scr, m_scr, acc_scr)
        return carry

    lax.fori_loop(0, i, body, 0)
    off = pl.multiple_of(i * t, t)
    _flash_tile(q, k_ref[0, pl.ds(off, t), :], v_ref[0, pl.ds(off, t), :], s_scr, p_scr, m_scr, acc_scr,
                bias_fn=lambda r0: _causal_bias(r0, 0, t))
    acc = acc_scr[...]
    o_ref[0] = (acc / acc[:, MLA_V:MLA_V + 1]).astype(BF16)


def _mla_attend(qcat, kcat, vcat, *, t=512):
    B, S, _ = qcat.shape
    t = min(t, S)
    H = MLA_HEADS
    kv = pl.BlockSpec((1, S, LANES), lambda b, h, i: (b, 0, h))
    return pl.pallas_call(
        functools.partial(_mla_kernel, t=t),
        grid=(B, H, S // t),
        in_specs=[pl.BlockSpec((1, t, LANES), lambda b, h, i: (b, i, h)), kv, kv],
        out_specs=pl.BlockSpec((1, t, LANES), lambda b, h, i: (b, i, h)),
        out_shape=jax.ShapeDtypeStruct((B, S, H * LANES), BF16),
        scratch_shapes=[pltpu.VMEM((t, t), F32), pltpu.VMEM((t, t), BF16),
                        pltpu.VMEM((t, LANES), F32), pltpu.VMEM((t, LANES), F32)],
        compiler_params=_params("parallel", "parallel", "arbitrary"),
        name="mla_attend",
    )(qcat, kcat, vcat)


def _merge_kernel(x_ref, g_ref, gm_ref, oa_ref, ob_ref, wa_ref, wb_ref, wo_ref, o_ref):
    D = x_ref.shape[2]
    gm = gm_ref[0]
    y = gm[:, :D] * _dot(oa_ref[0], wa_ref[...]) + gm[:, D:] * _dot(ob_ref[0], wb_ref[...])
    o_ref[0] = x_ref[0] + g_ref[0] * _dot(y.astype(BF16), wo_ref[...])


def _merge(x, g, gm, oa, ob, wa, wb, wo, *, tm=512):
    B, S, D = x.shape
    row = lambda w: pl.BlockSpec((1, tm, w), lambda b, i: (b, i, 0))
    res = lambda a: _resident(a.shape, lambda b, i: (0, 0))
    return pl.pallas_call(
        _merge_kernel,
        grid=(B, S // tm),
        in_specs=[row(D), pl.BlockSpec((1, 1, D), lambda b, i: (b, 0, 0)), row(2 * D),
                  row(oa.shape[2]), row(ob.shape[2]), res(wa), res(wb), res(wo)],
        out_specs=row(D),
        out_shape=jax.ShapeDtypeStruct((B, S, D), F32),
        compiler_params=_params("parallel", "parallel"),
        name="merge",
    )(x, g, gm, oa, ob, wa, wb, wo)


def _pad_heads(w, n_heads, width):
    k = w.shape[0]
    w = w.reshape(k, n_heads, width)
    return jnp.pad(w, ((0, 0), (0, 0), (0, LANES - width))).reshape(k, n_heads * LANES)


def _rot_cols(w):
    half = w.shape[-1] // 2
    return jnp.concatenate([-w[..., half:], w[..., :half]], axis=-1)


def _prep_inproj_weights(w_in, w_uq, w_uk, w_uv):
    D = w_in.shape[0]
    o = 0
    seg = {}
    for name, width in (("qa", 512), ("kc", 128), ("vc", 128), ("ks", 128), ("vs", 128), ("kw", 128),
                        ("vw", 128), ("gnsa", 3 * NSA_HEADS), ("cq", MLA_Q_RANK), ("ckv", MLA_KV_RANK),
                        ("kr", MLA_ROPE), ("gm", 2048)):
        seg[name] = w_in[:, o:o + width]
        o += width
    per_g = 3 * NSA_HPG
    gn = jnp.concatenate(
        [jnp.pad(seg["gnsa"][:, per_g * g:per_g * (g + 1)], ((0, 0), (0, LANES - per_g)))
         for g in range(NSA_GROUPS)], axis=1)
    z64 = jnp.zeros((D, MLA_NOPE), F32)
    z32 = jnp.zeros((D, LANES - MLA_NOPE - MLA_ROPE), F32)
    kr = jnp.concatenate([z64, seg["kr"], z32, z64, _rot_cols(seg["kr"]), z32], axis=1)
    w_all = jnp.concatenate([seg["qa"], seg["kc"], seg["vc"], seg["ks"], seg["vs"], seg["kw"], seg["vw"],
                             gn, seg["cq"], seg["ckv"], kr, seg["gm"]], axis=1).astype(BF16)
    assert w_all.shape[1] == _SEG_TOTAL
    hd = MLA_NOPE + MLA_ROPE
    uq = w_uq.reshape(-1, MLA_HEADS, hd)
    uq_rot = jnp.concatenate([jnp.zeros_like(uq[..., :MLA_NOPE]), _rot_cols(uq[..., MLA_NOPE:])], axis=-1)
    wqa = _pad_heads(w_uq, MLA_HEADS, hd).astype(BF16)
    wqb = _pad_heads(uq_rot.reshape(-1, MLA_HEADS * hd), MLA_HEADS, hd).astype(BF16)
    wuk = _pad_heads(w_uk, MLA_HEADS, MLA_NOPE).astype(BF16)
    wuv = _pad_heads(w_uv, MLA_HEADS, MLA_V).astype(BF16)
    return w_all, wqa, wqb, wuk, wuv


def _prep_compress_weights(k_w1, k_w2, k_pe, v_w1, v_w2, v_pe):
    d = NSA_D
    half = CMP_STRIDE

    def blockdiag(mats):
        n = len(mats)
        z = jnp.zeros_like(mats[0])
        rows = [jnp.concatenate([mats[r] if c == r else z for c in range(n)], axis=-1) for r in range(n)]
        return jnp.concatenate(rows, axis=-2)

    def first(w1, lo):
        return w1.reshape(CMP_BLOCK, d, d)[lo:lo + half]

    def w1_half(lo):
        k, v = first(k_w1, lo), first(v_w1, lo)
        return blockdiag([k, k, v, v]).reshape(half * 4 * d, 4 * d).astype(BF16)

    def pe_half(lo):
        k, v = k_pe[lo:lo + half], v_pe[lo:lo + half]
        return jnp.concatenate([k, k, v, v], axis=-1).reshape(1, half * 4 * d)

    w2 = blockdiag([k_w2, k_w2, v_w2, v_w2]).astype(BF16)
    return pe_half(0), pe_half(half), w1_half(0), w1_half(half), w2


def _rope_tables(S):
    half = MLA_ROPE // 2
    inv_freq = ROPE_THETA ** (-jnp.arange(half, dtype=F32) / half)
    ang = jnp.arange(S, dtype=F32)[:, None] * inv_freq[None, :]
    cos, sin = jnp.cos(ang), jnp.sin(ang)
    ones = jnp.ones((S, MLA_NOPE), F32)
    z_n = jnp.zeros((S, MLA_NOPE), F32)
    z_p = jnp.zeros((S, LANES - MLA_NOPE - MLA_ROPE), F32)
    ctab = jnp.concatenate([ones, cos, cos, z_p], axis=1)
    stab = jnp.concatenate([z_n, sin, sin, z_p], axis=1)
    return ctab, stab


def kernel(x, c, w_ada, b_ada, ffn1_gate, ffn1_up, ffn1_down, ffn2_gate, ffn2_up, ffn2_down, w_in, cmpk_w1, cmpk_w2, cmpk_pe, cmpv_w1, cmpv_w2, cmpv_pe, mla_q_norm, mla_w_uq, mla_kv_norm, mla_w_uk, mla_w_uv, w_branch_a, w_branch_b, w_out, final_norm):
    B, S, D = x.shape
    L = w_ada.shape[0]
    assert NSA_GROUPS == 2 and S // SLC_BLOCK <= LANES and S % 512 == 0
    n_top = min(N_SELECT, S // SLC_BLOCK)
    ctab, stab = _rope_tables(S)
    mod = _adaln(c, w_ada, b_ada).reshape(L, B, N_MOD, 1, D)
    bf = lambda w: w.astype(BF16)
    for l in range(L):
        sh1, sc1, g1, sh2, sc2, g2, sh3, sc3, g3 = (mod[l, :, k] for k in range(N_MOD))
        x = _ffn(x, sh1, sc1, g1, bf(ffn1_gate[l]), bf(ffn1_up[l]), bf(ffn1_down[l]), final_norm, final=False)

        w_all, wqa, wqb, wuk, wuv = _prep_inproj_weights(w_in[l], mla_w_uq[l], mla_w_uk[l], mla_w_uv[l])
        (qa, kcvc, ks, vs, kw, vw, gs, qcat, kcat, vcat, gm) = _inproj(
            x, sh2, sc2, w_all, wqa, wqb, wuk, wuv,
            mla_q_norm[l].reshape(1, -1), mla_kv_norm[l].reshape(1, -1), ctab, stab)
        kc, vct = _compress(kcvc, *_prep_compress_weights(
            cmpk_w1[l], cmpk_w2[l], cmpk_pe[l], cmpv_w1[l], cmpv_w2[l], cmpv_pe[l]))
        oc, qaug = _nsa_select(qa, kc, vct, n_top=n_top)
        oa = _nsa_attend(qaug, ks, vs, kw, vw, oc, gs)
        ob = _mla_attend(qcat, kcat, vcat)
        wb_pad = jnp.pad(w_branch_b[l].reshape(MLA_HEADS, MLA_V, D),
                         ((0, 0), (0, LANES - MLA_V), (0, 0))).reshape(MLA_HEADS * LANES, D)
        x = _merge(x, g2, gm, oa, ob, bf(w_branch_a[l]), bf(wb_pad), bf(w_out[l]))

        x = _ffn(x, sh3, sc3, g3, bf(ffn2_gate[l]), bf(ffn2_up[l]), bf(ffn2_down[l]), final_norm,
                 final=(l == L - 1))
    return x
```

```python
import functools
import math

import ml_dtypes
import numpy as np
import jax
import jax.numpy as jnp
from jax import lax
from jax.experimental import pallas as pl
from jax.experimental.pallas import tpu as pltpu

F32 = jnp.float32
BF16 = jnp.bfloat16

NSA_HEADS = 8
NSA_GROUPS = 2
NSA_HPG = NSA_HEADS // NSA_GROUPS
NSA_D = 64
CMP_BLOCK = 32
CMP_STRIDE = 16
SLC_BLOCK = 64
N_SELECT = 16
WINDOW = 512
MLA_HEADS = 8
MLA_NOPE = 64
MLA_ROPE = 32
MLA_V = 64
MLA_Q_RANK = 384
MLA_KV_RANK = 256
ROPE_THETA = 10000.0
N_MOD = 9
Q_BLOCK = 128
NORM_EPS = 1e-6

LANES = 128
SLC_SHIFT = SLC_BLOCK.bit_length() - 1
assert 1 << SLC_SHIFT == SLC_BLOCK
LOG2E = math.log2(math.e)
SLOPES = [2.0 ** (-8.0 * (k + 1) / NSA_HEADS) for k in range(NSA_HEADS)]
N_PIECES = 3


def _bf16_pieces(v):
    out = []
    for _ in range(N_PIECES):
        p = float(np.float32(v).astype(ml_dtypes.bfloat16))
        out.append(p)
        v -= p
    return out


SLOPE_PIECES = [_bf16_pieces(v * LOG2E) for v in SLOPES]
NEG = -1e30
VMEM_LIMIT = 56 * 1024 * 1024


def _rms(xf):
    return xf * lax.rsqrt(jnp.mean(xf * xf, -1, keepdims=True) + NORM_EPS)


def _sigmoid(x):
    return 1.0 / (1.0 + jnp.exp(-x))


def _dot(a, b):
    return jnp.dot(a, b, preferred_element_type=F32)


def _dot_nt(a, b, precision=None):
    return lax.dot_general(a, b, (((1,), (1,)), ((), ())), precision=precision,
                           preferred_element_type=F32)


def _params(*sem):
    return pltpu.CompilerParams(dimension_semantics=sem, vmem_limit_bytes=VMEM_LIMIT)


def _resident(shape, index_map):
    return pl.BlockSpec(shape, index_map, pipeline_mode=pl.Buffered(1))


def _adaln_kernel(c_ref, w_ref, b_ref, o_ref):
    c = c_ref[...]
    a = c * _sigmoid(c)
    o_ref[0] = jnp.dot(a, w_ref[0], precision=lax.Precision.HIGHEST,
                       preferred_element_type=F32) + b_ref[0]


def _adaln(c, w_ada, b_ada):
    L, D, N = w_ada.shape
    B = c.shape[0]
    rows = 8
    c_pad = jnp.zeros((rows, D), F32).at[:B].set(c)
    tn = N // 8
    out = pl.pallas_call(
        _adaln_kernel,
        grid=(L, N // tn),
        in_specs=[pl.BlockSpec((rows, D), lambda l, j: (0, 0)),
                  pl.BlockSpec((1, D, tn), lambda l, j: (l, 0, j)),
                  pl.BlockSpec((1, 1, tn), lambda l, j: (l, 0, j))],
        out_specs=pl.BlockSpec((1, rows, tn), lambda l, j: (l, 0, j)),
        out_shape=jax.ShapeDtypeStruct((L, rows, N), F32),
        compiler_params=_params("parallel", "parallel"),
        name="adaln_mod",
    )(c_pad, w_ada, b_ada.reshape(L, 1, N))
    return out[:, :B]


def _ffn_kernel(x_ref, sh_ref, sc_ref, g_ref, wg_ref, wu_ref, wd_ref, fn_ref, o_ref, *, tf, final):
    x = x_ref[0]
    u = (_rms(x) * (1.0 + sc_ref[0]) + sh_ref[0]).astype(BF16)
    n_ff = wg_ref.shape[1]
    acc = jnp.zeros(x.shape, F32)
    for f in range(0, n_ff, tf):
        hg = _dot(u, wg_ref[:, f:f + tf])
        hu = _dot(u, wu_ref[:, f:f + tf])
        a = (hg * _sigmoid(hg) * hu).astype(BF16)
        acc = acc + _dot(a, wd_ref[f:f + tf, :])
    y = x + (0.5 * g_ref[0]) * acc
    if final:
        y = _rms(y) * fn_ref[...]
    o_ref[0] = y


def _ffn(x, sh, sc, g, wg, wu, wd, final_norm, *, final, tm=512):
    B, S, D = x.shape
    n_ff = wg.shape[1]
    tf = n_ff // 2 if (n_ff // 2) % LANES == 0 else n_ff
    vec = pl.BlockSpec((1, 1, D), lambda b, i: (b, 0, 0))
    return pl.pallas_call(
        functools.partial(_ffn_kernel, tf=tf, final=final),
        grid=(B, S // tm),
        in_specs=[pl.BlockSpec((1, tm, D), lambda b, i: (b, i, 0)), vec, vec, vec,
                  _resident((D, n_ff), lambda b, i: (0, 0)),
                  _resident((D, n_ff), lambda b, i: (0, 0)),
                  _resident((n_ff, D), lambda b, i: (0, 0)),
                  pl.BlockSpec((1, D), lambda b, i: (0, 0))],
        out_specs=pl.BlockSpec((1, tm, D), lambda b, i: (b, i, 0)),
        out_shape=jax.ShapeDtypeStruct((B, S, D), F32),
        compiler_params=_params("parallel", "parallel"),
        name="ffn_final" if final else "ffn",
    )(x, sh, sc, g, wg, wu, wd, final_norm.reshape(1, D))


_SEG = {}
_o = 0
for _name, _w in (("qa", 512), ("kcvc", 256), ("kv4", 512), ("gnsa", 256), ("cq", MLA_Q_RANK),
                  ("ckv", MLA_KV_RANK), ("kr", 256), ("gm", 2048)):
    _SEG[_name] = (_o, _o + _w)
    _o += _w
_SEG_TOTAL = _o


def _inproj_kernel(x_ref, sh_ref, sc_ref, w_ref, wqa_ref, wqb_ref, wuk_ref, wuv_ref, qn_ref, kvn_ref,
                   ctab_ref, stab_ref,
                   qa_o, kcvc_o, ks_o, vs_o, kw_o, vw_o, gs_o, qcat_o, kcat_o, vcat_o, gm_o, *, tm):
    i = pl.program_id(1)
    x = x_ref[0]
    u = (_rms(x) * (1.0 + sc_ref[0]) + sh_ref[0]).astype(BF16)

    def proj(name):
        a, b = _SEG[name]
        return _dot(u, w_ref[:, a:b])

    qa_o[0] = proj("qa").astype(BF16)
    kcvc_o[0] = proj("kcvc")

    kv = proj("kv4")
    lane = lax.broadcasted_iota(jnp.int32, (tm, LANES), 1)
    pos = i * tm + lax.broadcasted_iota(jnp.int32, (tm, LANES), 0)
    blk = jnp.right_shift(pos, SLC_SHIFT)
    rem = pos - blk * SLC_BLOCK
    onehot = jnp.where(lane == blk, 1.0, 0.0).astype(BF16)
    zeros = jnp.zeros((tm, LANES), BF16)
    kextra = jnp.where(lane < NSA_D + N_PIECES, rem.astype(F32),
                       jnp.where(lane < NSA_D + 2 * N_PIECES, blk.astype(F32), 0.0))
    vextra = jnp.where(lane == NSA_D, 1.0, 0.0)
    for g in range(NSA_GROUPS):
        for src, k_o, v_o, first in ((0, ks_o, vs_o, onehot), (2, kw_o, vw_o, zeros)):
            kk = kv[:, LANES * src:LANES * (src + 1)]
            vv = kv[:, LANES * (src + 1):LANES * (src + 2)]
            if g == 1:
                kk = pltpu.roll(kk, NSA_D, 1)
                vv = pltpu.roll(vv, NSA_D, 1)
            k_o[0, g, :, 0:LANES] = first
            k_o[0, g, :, LANES:2 * LANES] = jnp.where(lane < NSA_D, kk, kextra).astype(BF16)
            v_o[0, g] = jnp.where(lane < NSA_D, vv, vextra).astype(BF16)

    gs = _sigmoid(proj("gnsa"))
    for g in range(NSA_GROUPS):
        gs_o[0, g] = gs[:, LANES * g:LANES * (g + 1)]

    ctab = ctab_ref[...]
    stab = stab_ref[...]
    scale = (MLA_NOPE + MLA_ROPE) ** -0.5 * LOG2E
    cqn =(_rms(proj("cq")) * qn_ref[...]).astype(BF16)
    q_a = _dot(cqn, wqa_ref[...])
    q_b = _dot(cqn, wqb_ref[...])
    ckvn = (_rms(proj("ckv")) * kvn_ref[...]).astype(BF16)
    kn = _dot(ckvn, wuk_ref[...])
    vv = _dot(ckvn, wuv_ref[...])
    kr = proj("kr")
    krope = kr[:, :LANES] * ctab + kr[:, LANES:] * stab
    for h in range(MLA_HEADS):
        sl = slice(LANES * h, LANES * (h + 1))
        qcat_o[0, :, sl] = ((q_a[:, sl] * ctab + q_b[:, sl] * stab) * scale).astype(BF16)
        kcat_o[0, :, sl] = (kn[:, sl] + krope).astype(BF16)
        vcat_o[0, :, sl] = jnp.where(lane == MLA_V, 1.0, vv[:, sl]).astype(BF16)

    gm_o[0] = _sigmoid(proj("gm"))


def _inproj(x, sh, sc, w_all, wqa, wqb, wuk, wuv, qn, kvn, ctab, stab, *, tm=512):
    B, S, D = x.shape
    G = NSA_GROUPS
    grid = (B, S // tm)
    vec = pl.BlockSpec((1, 1, D), lambda b, i: (b, 0, 0))

    def res(a):
        return _resident(a.shape, lambda b, i: (0,) * a.ndim)

    row = lambda w: pl.BlockSpec((1, tm, w), lambda b, i: (b, i, 0))
    grp = lambda w: pl.BlockSpec((1, G, tm, w), lambda b, i: (b, 0, i, 0))
    tab = pl.BlockSpec((tm, LANES), lambda b, i: (i, 0))
    out_shape = (
        jax.ShapeDtypeStruct((B, S, 512), BF16),
        jax.ShapeDtypeStruct((B, S, 256), F32),
        jax.ShapeDtypeStruct((B, G, S, 256), BF16),
        jax.ShapeDtypeStruct((B, G, S, 128), BF16),
        jax.ShapeDtypeStruct((B, G, S, 256), BF16),
        jax.ShapeDtypeStruct((B, G, S, 128), BF16),
        jax.ShapeDtypeStruct((B, G, S, 128), F32),
        jax.ShapeDtypeStruct((B, S, MLA_HEADS * LANES), BF16),
        jax.ShapeDtypeStruct((B, S, MLA_HEADS * LANES), BF16),
        jax.ShapeDtypeStruct((B, S, MLA_HEADS * LANES), BF16),
        jax.ShapeDtypeStruct((B, S, 2048), F32),
    )
    out_specs = (row(512), row(256), grp(256), grp(128), grp(256), grp(128), grp(128),
                 row(MLA_HEADS * LANES), row(MLA_HEADS * LANES), row(MLA_HEADS * LANES), row(2048))
    return pl.pallas_call(
        functools.partial(_inproj_kernel, tm=tm),
        grid=grid,
        in_specs=[pl.BlockSpec((1, tm, D), lambda b, i: (b, i, 0)), vec, vec,
                  res(w_all), res(wqa), res(wqb), res(wuk), res(wuv), res(qn), res(kvn), tab, tab],
        out_specs=out_specs,
        out_shape=out_shape,
        compiler_params=_params("parallel", "parallel"),
        name="in_proj",
    )(x, sh, sc, w_all, wqa, wqb, wuk, wuv, qn, kvn, ctab, stab)


def _compress_kernel(a_ref, pea_ref, peb_ref, wa_ref, wb_ref, w2_ref, kcat_o, vct_o):
    a = a_ref[0]
    n = a.shape[0]
    pa = _dot((a + pea_ref[...]).astype(BF16), wa_ref[...])
    pb = _dot((a + peb_ref[...]).astype(BF16), wb_ref[...])
    h = pa + pltpu.roll(pb, n - 1, 0)
    h = 0.5 * h * (1.0 + jnp.tanh(0.7978845608028654 * (h + 0.044715 * (h * h * h))))
    out = _dot(h.astype(BF16), w2_ref[...])
    out_t = out.T
    lane = lax.broadcasted_iota(jnp.int32, (n, LANES), 1)
    row = lax.broadcasted_iota(jnp.int32, (n, LANES), 0)
    pair = jnp.right_shift(row, 1)
    within = (row - 2 * pair) * CMP_STRIDE + (CMP_BLOCK - 1)
    kextra = jnp.where(lane < NSA_D + N_PIECES, pair.astype(F32),
                       jnp.where(lane < NSA_D + 2 * N_PIECES, within.astype(F32), 0.0))
    kk = out[:, 0:LANES] * (NSA_D ** -0.5 * LOG2E)
    for g in range(NSA_GROUPS):
        x = kk if g == 0 else pltpu.roll(kk, NSA_D, 1)
        lo = x - x.astype(BF16).astype(F32)
        kcat_o[0, g, :, 0:LANES] = jnp.where(lane < NSA_D, x, kextra).astype(BF16)
        kcat_o[0, g, :, LANES:2 * LANES] = jnp.where(lane < NSA_D, lo, 0.0).astype(BF16)
        vct_o[0, g] = out_t[2 * NSA_D + NSA_D * g:2 * NSA_D + NSA_D * (g + 1), :].astype(BF16)


def _compress(kcvc, pea, peb, wa, wb, w2):
    B, S, W = kcvc.shape
    n = S // CMP_STRIDE
    a = kcvc.reshape(B, n, CMP_STRIDE * W)
    G = NSA_GROUPS
    full = lambda arr: _resident(arr.shape, lambda b: (0,) * arr.ndim)
    return pl.pallas_call(
        _compress_kernel,
        grid=(B,),
        in_specs=[pl.BlockSpec((1, n, CMP_STRIDE * W), lambda b: (b, 0, 0)),
                  full(pea), full(peb), full(wa), full(wb), full(w2)],
        out_specs=(pl.BlockSpec((1, G, n, 2 * LANES), lambda b: (b, 0, 0, 0)),
                   pl.BlockSpec((1, G, NSA_D, n), lambda b: (b, 0, 0, 0))),
        out_shape=(jax.ShapeDtypeStruct((B, G, n, 2 * LANES), BF16),
                   jax.ShapeDtypeStruct((B, G, NSA_D, n), BF16)),
        compiler_params=_params("parallel"),
        name="compress",
    )(a, pea, peb, wa, wb, w2)


def _nsa_select_kernel(q_ref, kcat_ref, vct_ref, oc_o, qaug_o, qc_scr, s_scr, p_scr, *, n_top):
    g = pl.program_id(1)
    i = pl.program_id(2)
    tq = q_ref.shape[1]
    ncp = kcat_ref.shape[2]
    q0 = i * tq
    scale = NSA_D ** -0.5

    lane = lax.broadcasted_iota(jnp.int32, (tq, LANES), 1)
    qf = q_ref[0].astype(F32)
    for h in range(NSA_HPG):
        rs = slice(tq * h, tq * (h + 1))
        x = qf[:, LANES * (h // 2):LANES * (h // 2 + 1)]
        if h % 2 == 1:
            x = pltpu.roll(x, NSA_D, 1)
        extra = jnp.zeros((tq, LANES), F32)
        extra_c = jnp.zeros((tq, LANES), F32)
        for k in range(N_PIECES):
            piece = jnp.where(g == 0, SLOPE_PIECES[h][k], SLOPE_PIECES[NSA_HPG + h][k])
            extra = jnp.where(lane == NSA_D + k, piece,
                              jnp.where(lane == NSA_D + N_PIECES + k, piece * SLC_BLOCK, extra))
            extra_c = jnp.where(lane == NSA_D + k, piece * (2 * CMP_STRIDE),
                                jnp.where(lane == NSA_D + N_PIECES + k, piece, extra_c))
        qaug_o[0, 0, 0, rs, LANES:2 * LANES] = jnp.where(lane < NSA_D, x * (scale * LOG2E), extra).astype(BF16)
        qc_scr[rs, 0:LANES] = jnp.where(lane < NSA_D, x, extra_c).astype(BF16)
        qc_scr[rs, LANES:2 * LANES] = jnp.where(lane < NSA_D, x, 0.0).astype(BF16)

    s_scr[...] = _dot_nt(kcat_ref[0, 0], qc_scr[...])
    cmp_end = lax.broadcasted_iota(jnp.int32, (ncp, tq), 0) * CMP_STRIDE + (CMP_BLOCK - 1)
    t_c = q0 + lax.broadcasted_iota(jnp.int32, (ncp, tq), 1)
    mask_c = cmp_end <= t_c
    psum = jnp.zeros((ncp, tq), F32)
    inv = []
    for h in range(NSA_HPG):
        cs = slice(tq * h, tq * (h + 1))
        s = jnp.where(mask_c, s_scr[:, cs], -jnp.inf)
        m = jnp.max(s, axis=0, keepdims=True)
        m = jnp.where(m == -jnp.inf, 0.0, m)
        p = jnp.exp2(s - m)
        r = 1.0 / jnp.maximum(jnp.sum(p, axis=0, keepdims=True), 1.0)
        p_scr[:, cs] = p.astype(BF16)
        psum = psum + p * r
        inv.append(r)
    oc_t = _dot(vct_ref[0, 0], p_scr[...])
    oc_o[0] = jnp.concatenate([oc_t[:, tq * h:tq * (h + 1)] * inv[h] for h in range(NSA_HPG)], axis=0).T

    ratio = SLC_BLOCK // CMP_STRIDE
    r = CMP_BLOCK // CMP_STRIDE
    bb = lax.broadcasted_iota(jnp.int32, (LANES, ncp), 0)
    mm = lax.broadcasted_iota(jnp.int32, (LANES, ncp), 1)
    a_t = jnp.where((mm >= ratio * bb - (r - 1)) & (mm <= ratio * bb + ratio - 1), 1.0, 0.0).astype(BF16)
    p_hi = psum.astype(BF16)
    p_lo = (psum - p_hi.astype(F32)).astype(BF16)
    imp = _dot(a_t, p_hi) + _dot(a_t, p_lo)

    blk = lax.broadcasted_iota(jnp.int32, (LANES, tq), 0)
    t_s = q0 + lax.broadcasted_iota(jnp.int32, (LANES, tq), 1)
    cur = jnp.right_shift(t_s, SLC_SHIFT)
    forced = (blk == 0) | (blk == cur) | (blk == cur - 1)
    valid = blk * SLC_BLOCK <= t_s
    cand = jnp.where(forced, jnp.inf, jnp.where(valid, imp, -jnp.inf))
    blkf = blk.astype(F32)

    def pick_one(_, carry):
        cand, sel = carry
        m = jnp.max(cand, axis=0, keepdims=True)
        idx = jnp.min(jnp.where(cand == m, blkf, float(LANES)), axis=0, keepdims=True)
        pick = blkf == idx
        return jnp.where(pick, -jnp.inf, cand), jnp.where(pick, 1.0, sel)

    _, sel = lax.fori_loop(0, n_top, pick_one, (cand, jnp.zeros((LANES, tq), F32)))
    pen_t = jnp.where(valid, jnp.where(sel > 0.0, 0.0, NEG), NEG)
    pen = pen_t.T.astype(BF16)
    for h in range(NSA_HPG):
        qaug_o[0, 0, 0, tq * h:tq * (h + 1), 0:LANES] = pen


def _nsa_select(qa, kc, vct, *, n_top):
    B, S, _ = qa.shape
    G = NSA_GROUPS
    tq = Q_BLOCK
    nq = S // tq
    ncp = kc.shape[2]
    gw = NSA_HPG * NSA_D
    return pl.pallas_call(
        functools.partial(_nsa_select_kernel, n_top=n_top),
        grid=(B, G, nq),
        in_specs=[pl.BlockSpec((1, tq, gw), lambda b, g, i: (b, i, g)),
                  pl.BlockSpec((1, 1, ncp, 2 * LANES), lambda b, g, i: (b, g, 0, 0)),
                  pl.BlockSpec((1, 1, NSA_D, ncp), lambda b, g, i: (b, g, 0, 0))],
        out_specs=(pl.BlockSpec((1, tq, gw), lambda b, g, i: (b, i, g)),
                   pl.BlockSpec((1, 1, 1, NSA_HPG * tq, 2 * LANES), lambda b, g, i: (b, g, i, 0, 0))),
        out_shape=(jax.ShapeDtypeStruct((B, S, G * gw), F32),
                   jax.ShapeDtypeStruct((B, G, nq, NSA_HPG * tq, 2 * LANES), BF16)),
        scratch_shapes=[pltpu.VMEM((NSA_HPG * tq, 2 * LANES), BF16),
                        pltpu.VMEM((ncp, NSA_HPG * tq), F32), pltpu.VMEM((ncp, NSA_HPG * tq), BF16)],
        compiler_params=_params("parallel", "parallel", "parallel"),
        name="nsa_select",
    )(qa, kc, vct)


CHUNK = 64


def _scores(q, k, s_scr):
    s_scr[:, :k.shape[0]] = _dot_nt(q, k)


def _softmax_pv(s_scr, p_scr, v, m_scr, acc_scr, bias_fn=None):
    rows, n = m_scr.shape[0], v.shape[0]
    for r0 in range(0, rows, CHUNK):
        rs = slice(r0, r0 + CHUNK)
        s = s_scr[rs, :n]
        if bias_fn is not None:
            s = s + bias_fn(r0)
        m_old = m_scr[rs]
        m_new = jnp.maximum(m_old, jnp.broadcast_to(jnp.max(s, axis=-1, keepdims=True), m_old.shape))
        m_scr[rs] = m_new
        acc_scr[rs] = acc_scr[rs] * jnp.exp2(m_old - m_new)
        for c in range(0, n, LANES):
            p_scr[rs, c:c + LANES] = jnp.exp2(s[:, c:c + LANES] - m_new).astype(BF16)
    acc_scr[...] += _dot(p_scr[:, :n], v)


def _causal_flash(q, k_ref, v_ref, n_full, tk, s_a, s_b, p_a, p_b, m_scr, acc_scr, tail_bias):
    k_tile = lambda j: k_ref[pl.ds(pl.multiple_of(j * tk, tk), tk), :]
    v_tile = lambda j: v_ref[pl.ds(pl.multiple_of(j * tk, tk), tk), :]
    n_pairs = lax.div(n_full, 2)

    _scores(q, k_tile(0), s_a)

    def pair(jj, carry):
        j = 2 * jj
        _scores(q, k_tile(j + 1), s_b)
        _softmax_pv(s_a, p_a, v_tile(j), m_scr, acc_scr)
        _scores(q, k_tile(j + 2), s_a)
        _softmax_pv(s_b, p_b, v_tile(j + 1), m_scr, acc_scr)
        return carry

    lax.fori_loop(0, n_pairs, pair, 0)
    j = 2 * n_pairs

    @pl.when(j < n_full)
    def _():
        _scores(q, k_tile(j + 1), s_b)
        _softmax_pv(s_a, p_a, v_tile(j), m_scr, acc_scr)
        _softmax_pv(s_b, p_b, v_tile(j + 1), m_scr, acc_scr, bias_fn=tail_bias)

    @pl.when(j == n_full)
    def _():
        _softmax_pv(s_a, p_a, v_tile(j), m_scr, acc_scr, bias_fn=tail_bias)


def _causal_bias(t0, pos0, n):
    t = t0 + lax.broadcasted_iota(jnp.int32, (CHUNK, n), 0)
    pos = pos0 + lax.broadcasted_iota(jnp.int32, (CHUNK, n), 1)
    return jnp.where(pos <= t, 0.0, NEG)


def _nsa_attend_kernel(qaug_ref, ks_ref, vs_ref, kw_ref, vw_ref, oc_ref, gs_ref, o_ref,
                       s_scr, s_b, p_scr, p_b, m_scr, accs_scr, accw_scr, *, tk):
    i = pl.program_id(2)
    tq = oc_ref.shape[1]
    rows = NSA_HPG * tq
    q = qaug_ref[0, 0, 0]
    q0 = i * tq
    row_t0 = lambda r0: q0 + r0 % tq

    m_scr[...] = jnp.full((rows, LANES), NEG, F32)
    accs_scr[...] = jnp.zeros((rows, LANES), F32)
    n_full = lax.div(q0, tk)
    _causal_flash(q, ks_ref.at[0, 0], vs_ref.at[0, 0], n_full, tk, s_scr, s_b, p_scr, p_b, m_scr, accs_scr,
                  tail_bias=lambda r0: _causal_bias(row_t0(r0), n_full * tk, tk))

    nw = WINDOW + tq
    offw = pl.multiple_of(jnp.maximum(q0 - WINDOW, 0), tq)

    def window_bias(r0):
        d = (row_t0(r0) + lax.broadcasted_iota(jnp.int32, (CHUNK, nw), 0)
             - (offw + lax.broadcasted_iota(jnp.int32, (CHUNK, nw), 1)))
        return jnp.where(d >= 0, jnp.where(d < WINDOW, 0.0, NEG), NEG)

    m_scr[...] = jnp.full((rows, LANES), NEG, F32)
    accw_scr[...] = jnp.zeros((rows, LANES), F32)
    _scores(q, kw_ref[0, 0, pl.ds(offw, nw), :], s_scr)
    _softmax_pv(s_scr, p_scr, vw_ref[0, 0, pl.ds(offw, nw), :], m_scr, accw_scr, bias_fn=window_bias)

    gs = gs_ref[0, 0]
    for h in range(NSA_HPG):
        rs = slice(tq * h, tq * (h + 1))
        acc_s = accs_scr[rs]
        acc_w = accw_scr[rs]
        o_s = acc_s[:, 0:NSA_D] / acc_s[:, NSA_D:NSA_D + 1]
        o_w = acc_w[:, 0:NSA_D] / acc_w[:, NSA_D:NSA_D + 1]
        o_c = oc_ref[0, :, NSA_D * h:NSA_D * (h + 1)]
        o = (gs[:, 3 * h:3 * h + 1] * o_c + gs[:, 3 * h + 1:3 * h + 2] * o_s
             + gs[:, 3 * h + 2:3 * h + 3] * o_w)
        o_ref[0, :, NSA_D * h:NSA_D * (h + 1)] = o.astype(BF16)


def _nsa_attend(qaug, ks, vs, kw, vw, oc, gs, *, tk=512):
    B, G, nq = qaug.shape[:3]
    S = ks.shape[2]
    tq = Q_BLOCK
    tk = min(tk, S)
    gw = NSA_HPG * NSA_D
    rows = NSA_HPG * tq
    wide = max(tk, WINDOW + tq)
    assert S >= WINDOW + tq
    kv = lambda w: pl.BlockSpec((1, 1, S, w), lambda b, g, i: (b, g, 0, 0))
    return pl.pallas_call(
        functools.partial(_nsa_attend_kernel, tk=tk),
        grid=(B, G, nq),
        in_specs=[pl.BlockSpec((1, 1, 1, NSA_HPG * tq, 2 * LANES), lambda b, g, i: (b, g, i, 0, 0)),
                  kv(2 * LANES), kv(LANES), kv(2 * LANES), kv(LANES),
                  pl.BlockSpec((1, tq, gw), lambda b, g, i: (b, i, g)),
                  pl.BlockSpec((1, 1, tq, LANES), lambda b, g, i: (b, g, i, 0))],
        out_specs=pl.BlockSpec((1, tq, gw), lambda b, g, i: (b, i, g)),
        out_shape=jax.ShapeDtypeStruct((B, S, G * gw), BF16),
        scratch_shapes=[pltpu.VMEM((rows, wide), F32), pltpu.VMEM((rows, tk), F32),
                        pltpu.VMEM((rows, wide), BF16), pltpu.VMEM((rows, tk), BF16),
                        pltpu.VMEM((rows, LANES), F32), pltpu.VMEM((rows, LANES), F32),
                        pltpu.VMEM((rows, LANES), F32)],
        compiler_params=_params("parallel", "parallel", "arbitrary"),
        name="nsa_attend",
    )(qaug, ks, vs, kw, vw, oc, gs)


def _mla_kernel(q_ref, k_ref, v_ref, o_ref, s_a, s_b, p_a, p_b, m_scr, acc_scr, *, t):
    i = pl.program_id(2)
    q = q_ref[0]
    m_scr[...] = jnp.full((t, LANES), NEG, F32)
    acc_scr[...] = jnp.zeros((t, LANES), F32)
    _causal_flash(q, k_ref.at[0], v_ref.at[0], i, t, s_a, s_b, p_a, p_b, m_scr, acc_scr,
                  tail_bias=lambda r0: _causal_bias(r0, 0, t))
    acc = acc_scr[...]
    o_ref[0] = (acc / acc[:, MLA_V:MLA_V + 1]).astype(BF16)


def _mla_attend(qcat, kcat, vcat, *, t=512):
    B, S, _ = qcat.shape
    t = min(t, S)
    H = MLA_HEADS
    kv = pl.BlockSpec((1, S, LANES), lambda b, h, i: (b, 0, h))
    return pl.pallas_call(
        functools.partial(_mla_kernel, t=t),
        grid=(B, H, S // t),
        in_specs=[pl.BlockSpec((1, t, LANES), lambda b, h, i: (b, i, h)), kv, kv],
        out_specs=pl.BlockSpec((1, t, LANES), lambda b, h, i: (b, i, h)),
        out_shape=jax.ShapeDtypeStruct((B, S, H * LANES), BF16),
        scratch_shapes=[pltpu.VMEM((t, t), F32), pltpu.VMEM((t, t), F32),
                        pltpu.VMEM((t, t), BF16), pltpu.VMEM((t, t), BF16),
                        pltpu.VMEM((t, LANES), F32), pltpu.VMEM((t, LANES), F32)],
        compiler_params=_params("parallel", "parallel", "arbitrary"),
        name="mla_attend",
    )(qcat, kcat, vcat)


def _merge_kernel(x_ref, g_ref, gm_ref, oa_ref, ob_ref, wa_ref, wb_ref, wo_ref, o_ref):
    D = x_ref.shape[2]
    gm = gm_ref[0]
    y = gm[:, :D] * _dot(oa_ref[0], wa_ref[...]) + gm[:, D:] * _dot(ob_ref[0], wb_ref[...])
    o_ref[0] = x_ref[0] + g_ref[0] * _dot(y.astype(BF16), wo_ref[...])


def _merge(x, g, gm, oa, ob, wa, wb, wo, *, tm=512):
    B, S, D = x.shape
    row = lambda w: pl.BlockSpec((1, tm, w), lambda b, i: (b, i, 0))
    res = lambda a: _resident(a.shape, lambda b, i: (0, 0))
    return pl.pallas_call(
        _merge_kernel,
        grid=(B, S // tm),
        in_specs=[row(D), pl.BlockSpec((1, 1, D), lambda b, i: (b, 0, 0)), row(2 * D),
                  row(oa.shape[2]), row(ob.shape[2]), res(wa), res(wb), res(wo)],
        out_specs=row(D),
        out_shape=jax.ShapeDtypeStruct((B, S, D), F32),
        compiler_params=_params("parallel", "parallel"),
        name="merge",
    )(x, g, gm, oa, ob, wa, wb, wo)


def _pad_heads(w, n_heads, width):
    k = w.shape[0]
    w = w.reshape(k, n_heads, width)
    return jnp.pad(w, ((0, 0), (0, 0), (0, LANES - width))).reshape(k, n_heads * LANES)


def _rot_cols(w):
    half = w.shape[-1] // 2
    return jnp.concatenate([-w[..., half:], w[..., :half]], axis=-1)


def _prep_inproj_weights(w_in, w_uq, w_uk, w_uv):
    D = w_in.shape[0]
    o = 0
    seg = {}
    for name, width in (("qa", 512), ("kc", 128), ("vc", 128), ("ks", 128), ("vs", 128), ("kw", 128),
                        ("vw", 128), ("gnsa", 3 * NSA_HEADS), ("cq", MLA_Q_RANK), ("ckv", MLA_KV_RANK),
                        ("kr", MLA_ROPE), ("gm", 2048)):
        seg[name] = w_in[:, o:o + width]
        o += width
    per_g = 3 * NSA_HPG
    gn = jnp.concatenate(
        [jnp.pad(seg["gnsa"][:, per_g * g:per_g * (g + 1)], ((0, 0), (0, LANES - per_g)))
         for g in range(NSA_GROUPS)], axis=1)
    z64 = jnp.zeros((D, MLA_NOPE), F32)
    z32 = jnp.zeros((D, LANES - MLA_NOPE - MLA_ROPE), F32)
    kr = jnp.concatenate([z64, seg["kr"], z32, z64, _rot_cols(seg["kr"]), z32], axis=1)
    w_all = jnp.concatenate([seg["qa"], seg["kc"], seg["vc"], seg["ks"], seg["vs"], seg["kw"], seg["vw"],
                             gn, seg["cq"], seg["ckv"], kr, seg["gm"]], axis=1).astype(BF16)
    assert w_all.shape[1] == _SEG_TOTAL
    hd = MLA_NOPE + MLA_ROPE
    uq = w_uq.reshape(-1, MLA_HEADS, hd)
    uq_rot = jnp.concatenate([jnp.zeros_like(uq[..., :MLA_NOPE]), _rot_cols(uq[..., MLA_NOPE:])], axis=-1)
    wqa = _pad_heads(w_uq, MLA_HEADS, hd).astype(BF16)
    wqb = _pad_heads(uq_rot.reshape(-1, MLA_HEADS * hd), MLA_HEADS, hd).astype(BF16)
    wuk = _pad_heads(w_uk, MLA_HEADS, MLA_NOPE).astype(BF16)
    wuv = _pad_heads(w_uv, MLA_HEADS, MLA_V).astype(BF16)
    return w_all, wqa, wqb, wuk, wuv


def _prep_compress_weights(k_w1, k_w2, k_pe, v_w1, v_w2, v_pe):
    d = NSA_D
    half = CMP_STRIDE

    def blockdiag(mats):
        n = len(mats)
        z = jnp.zeros_like(mats[0])
        rows = [jnp.concatenate([mats[r] if c == r else z for c in range(n)], axis=-1) for r in range(n)]
        return jnp.concatenate(rows, axis=-2)

    def first(w1, lo):
        return w1.reshape(CMP_BLOCK, d, d)[lo:lo + half]

    def w1_half(lo):
        k, v = first(k_w1, lo), first(v_w1, lo)
        return blockdiag([k, k, v, v]).reshape(half * 4 * d, 4 * d).astype(BF16)

    def pe_half(lo):
        k, v = k_pe[lo:lo + half], v_pe[lo:lo + half]
        return jnp.concatenate([k, k, v, v], axis=-1).reshape(1, half * 4 * d)

    w2 = blockdiag([k_w2, k_w2, v_w2, v_w2]).astype(BF16)
    return pe_half(0), pe_half(half), w1_half(0), w1_half(half), w2


def _rope_tables(S):
    half = MLA_ROPE // 2
    inv_freq = ROPE_THETA ** (-jnp.arange(half, dtype=F32) / half)
    ang = jnp.arange(S, dtype=F32)[:, None] * inv_freq[None, :]
    cos, sin = jnp.cos(ang), jnp.sin(ang)
    ones = jnp.ones((S, MLA_NOPE), F32)
    z_n = jnp.zeros((S, MLA_NOPE), F32)
    z_p = jnp.zeros((S, LANES - MLA_NOPE - MLA_ROPE), F32)
    ctab = jnp.concatenate([ones, cos, cos, z_p], axis=1)
    stab = jnp.concatenate([z_n, sin, sin, z_p], axis=1)
    return ctab, stab


def kernel(x, c, w_ada, b_ada, ffn1_gate, ffn1_up, ffn1_down, ffn2_gate, ffn2_up, ffn2_down, w_in, cmpk_w1, cmpk_w2, cmpk_pe, cmpv_w1, cmpv_w2, cmpv_pe, mla_q_norm, mla_w_uq, mla_kv_norm, mla_w_uk, mla_w_uv, w_branch_a, w_branch_b, w_out, final_norm):
    B, S, D = x.shape
    L = w_ada.shape[0]
    assert NSA_GROUPS == 2 and S // SLC_BLOCK <= LANES and S // (2 * CMP_STRIDE) <= 256 and S % 512 == 0
    n_top = min(N_SELECT, S // SLC_BLOCK)
    ctab, stab = _rope_tables(S)
    mod = _adaln(c, w_ada, b_ada).reshape(L, B, N_MOD, 1, D)
    bf = lambda w: w.astype(BF16)
    for l in range(L):
        sh1, sc1, g1, sh2, sc2, g2, sh3, sc3, g3 = (mod[l, :, k] for k in range(N_MOD))
        x = _ffn(x, sh1, sc1, g1, bf(ffn1_gate[l]), bf(ffn1_up[l]), bf(ffn1_down[l]), final_norm, final=False)

        w_all, wqa, wqb, wuk, wuv = _prep_inproj_weights(w_in[l], mla_w_uq[l], mla_w_uk[l], mla_w_uv[l])
        (qa, kcvc, ks, vs, kw, vw, gs, qcat, kcat, vcat, gm) = _inproj(
            x, sh2, sc2, w_all, wqa, wqb, wuk, wuv,
            mla_q_norm[l].reshape(1, -1), mla_kv_norm[l].reshape(1, -1), ctab, stab)
        kc, vct = _compress(kcvc, *_prep_compress_weights(
            cmpk_w1[l], cmpk_w2[l], cmpk_pe[l], cmpv_w1[l], cmpv_w2[l], cmpv_pe[l]))
        oc, qaug = _nsa_select(qa, kc, vct, n_top=n_top)
        oa = _nsa_attend(qaug, ks, vs, kw, vw, oc, gs)
        ob = _mla_attend(qcat, kcat, vcat)
        wb_pad = jnp.pad(w_branch_b[l].reshape(MLA_HEADS, MLA_V, D),
                         ((0, 0), (0, LANES - MLA_V), (0, 0))).reshape(MLA_HEADS * LANES, D)
        x = _merge(x, g2, gm, oa, ob, bf(w_branch_a[l]), bf(wb_pad), bf(w_out[l]))

        x = _ffn(x, sh3, sc3, g3, bf(ffn2_gate[l]), bf(ffn2_up[l]), bf(ffn2_down[l]), final_norm,
                 final=(l == L - 1))
    return x
```

```python
import functools
import math

import ml_dtypes
import numpy as np
import jax
import jax.numpy as jnp
from jax import lax
from jax.experimental import pallas as pl
from jax.experimental.pallas import tpu as pltpu

F32 = jnp.float32
BF16 = jnp.bfloat16

NSA_HEADS = 8
NSA_GROUPS = 2
NSA_HPG = NSA_HEADS // NSA_GROUPS
NSA_D = 64
CMP_BLOCK = 32
CMP_STRIDE = 16
SLC_BLOCK = 64
N_SELECT = 16
WINDOW = 512
MLA_HEADS = 8
MLA_NOPE = 64
MLA_ROPE = 32
MLA_V = 64
MLA_Q_RANK = 384
MLA_KV_RANK = 256
ROPE_THETA = 10000.0
N_MOD = 9
Q_BLOCK = 128
NORM_EPS = 1e-6

LANES = 128
SLC_SHIFT = SLC_BLOCK.bit_length() - 1
assert 1 << SLC_SHIFT == SLC_BLOCK
LOG2E = math.log2(math.e)
SLOPES = [2.0 ** (-8.0 * (k + 1) / NSA_HEADS) for k in range(NSA_HEADS)]
N_PIECES = 3


def _bf16_pieces(v):
    out = []
    for _ in range(N_PIECES):
        p = float(np.float32(v).astype(ml_dtypes.bfloat16))
        out.append(p)
        v -= p
    return out


SLOPE_PIECES = [_bf16_pieces(v * LOG2E) for v in SLOPES]
NEG = -1e30
VMEM_LIMIT = 56 * 1024 * 1024


def _rms(xf):
    return xf * lax.rsqrt(jnp.mean(xf * xf, -1, keepdims=True) + NORM_EPS)


def _sigmoid(x):
    return 1.0 / (1.0 + jnp.exp(-x))


def _dot(a, b):
    return jnp.dot(a, b, preferred_element_type=F32)


def _dot_nt(a, b, precision=None):
    return lax.dot_general(a, b, (((1,), (1,)), ((), ())), precision=precision,
                           preferred_element_type=F32)


def _params(*sem):
    return pltpu.CompilerParams(dimension_semantics=sem, vmem_limit_bytes=VMEM_LIMIT)


def _resident(shape, index_map):
    return pl.BlockSpec(shape, index_map, pipeline_mode=pl.Buffered(1))


def _adaln_kernel(c_ref, w_ref, b_ref, o_ref):
    c = c_ref[...]
    a = c * _sigmoid(c)
    o_ref[0] = jnp.dot(a, w_ref[0], precision=lax.Precision.HIGHEST,
                       preferred_element_type=F32) + b_ref[0]


def _adaln(c, w_ada, b_ada):
    L, D, N = w_ada.shape
    B = c.shape[0]
    rows = 8
    c_pad = jnp.zeros((rows, D), F32).at[:B].set(c)
    tn = N // 8
    out = pl.pallas_call(
        _adaln_kernel,
        grid=(L, N // tn),
        in_specs=[pl.BlockSpec((rows, D), lambda l, j: (0, 0)),
                  pl.BlockSpec((1, D, tn), lambda l, j: (l, 0, j)),
                  pl.BlockSpec((1, 1, tn), lambda l, j: (l, 0, j))],
        out_specs=pl.BlockSpec((1, rows, tn), lambda l, j: (l, 0, j)),
        out_shape=jax.ShapeDtypeStruct((L, rows, N), F32),
        compiler_params=_params("parallel", "parallel"),
        name="adaln_mod",
    )(c_pad, w_ada, b_ada.reshape(L, 1, N))
    return out[:, :B]


def _ffn_kernel(x_ref, sh_ref, sc_ref, g_ref, wg_ref, wu_ref, wd_ref, fn_ref, o_ref, *, tf, final):
    x = x_ref[0]
    u = (_rms(x) * (1.0 + sc_ref[0]) + sh_ref[0]).astype(BF16)
    n_ff = wg_ref.shape[1]
    acc = jnp.zeros(x.shape, F32)
    for f in range(0, n_ff, tf):
        hg = _dot(u, wg_ref[:, f:f + tf])
        hu = _dot(u, wu_ref[:, f:f + tf])
        a = (hg * _sigmoid(hg) * hu).astype(BF16)
        acc = acc + _dot(a, wd_ref[f:f + tf, :])
    y = x + (0.5 * g_ref[0]) * acc
    if final:
        y = _rms(y) * fn_ref[...]
    o_ref[0] = y


def _ffn(x, sh, sc, g, wg, wu, wd, final_norm, *, final, tm=512):
    B, S, D = x.shape
    n_ff = wg.shape[1]
    tf = n_ff // 2 if (n_ff // 2) % LANES == 0 else n_ff
    vec = pl.BlockSpec((1, 1, D), lambda b, i: (b, 0, 0))
    return pl.pallas_call(
        functools.partial(_ffn_kernel, tf=tf, final=final),
        grid=(B, S // tm),
        in_specs=[pl.BlockSpec((1, tm, D), lambda b, i: (b, i, 0)), vec, vec, vec,
                  _resident((D, n_ff), lambda b, i: (0, 0)),
                  _resident((D, n_ff), lambda b, i: (0, 0)),
                  _resident((n_ff, D), lambda b, i: (0, 0)),
                  pl.BlockSpec((1, D), lambda b, i: (0, 0))],
        out_specs=pl.BlockSpec((1, tm, D), lambda b, i: (b, i, 0)),
        out_shape=jax.ShapeDtypeStruct((B, S, D), F32),
        compiler_params=_params("parallel", "parallel"),
        name="ffn_final" if final else "ffn",
    )(x, sh, sc, g, wg, wu, wd, final_norm.reshape(1, D))


_SEG = {}
_o = 0
for _name, _w in (("qa", 512), ("kcvc", 256), ("kv4", 512), ("gnsa", 256), ("cq", MLA_Q_RANK),
                  ("ckv", MLA_KV_RANK), ("kr", 256), ("gm", 2048)):
    _SEG[_name] = (_o, _o + _w)
    _o += _w
_SEG_TOTAL = _o


def _inproj_kernel(x_ref, sh_ref, sc_ref, w_ref, wqa_ref, wqb_ref, wuk_ref, wuv_ref, qn_ref, kvn_ref,
                   ctab_ref, stab_ref,
                   qa_o, kcvc_o, ks_o, vs_o, kw_o, vw_o, gs_o, qcat_o, kcat_o, vcat_o, gm_o, *, tm):
    i = pl.program_id(1)
    x = x_ref[0]
    u = (_rms(x) * (1.0 + sc_ref[0]) + sh_ref[0]).astype(BF16)

    def proj(name):
        a, b = _SEG[name]
        return _dot(u, w_ref[:, a:b])

    qa_o[0] = proj("qa").astype(BF16)
    kcvc_o[0] = proj("kcvc")

    kv = proj("kv4")
    lane = lax.broadcasted_iota(jnp.int32, (tm, LANES), 1)
    pos = i * tm + lax.broadcasted_iota(jnp.int32, (tm, LANES), 0)
    blk = jnp.right_shift(pos, SLC_SHIFT)
    rem = pos - blk * SLC_BLOCK
    onehot = jnp.where(lane == blk, 1.0, 0.0).astype(BF16)
    zeros = jnp.zeros((tm, LANES), BF16)
    kextra = jnp.where(lane < NSA_D + N_PIECES, rem.astype(F32),
                       jnp.where(lane < NSA_D + 2 * N_PIECES, blk.astype(F32), 0.0))
    for g in range(NSA_GROUPS):
        for src, k_o, v_o, first in ((0, ks_o, vs_o, onehot), (2, kw_o, vw_o, zeros)):
            kk = kv[:, LANES * src:LANES * (src + 1)]
            vv = kv[:, LANES * (src + 1):LANES * (src + 2)]
            vr = pltpu.roll(vv, NSA_D, 1)
            v_lo, v_hi = (vv, vr) if g == 0 else (vr, vv)
            if g == 1:
                kk = pltpu.roll(kk, NSA_D, 1)
            k_o[0, g, :, 0:LANES] = first
            k_o[0, g, :, LANES:2 * LANES] = jnp.where(lane < NSA_D, kk, kextra).astype(BF16)
            v_o[0, g, :, 0:LANES] = jnp.where(lane < NSA_D, v_lo, 1.0).astype(BF16)
            v_o[0, g, :, LANES:2 * LANES] = jnp.where(lane < NSA_D, 1.0, v_hi).astype(BF16)

    gs = _sigmoid(proj("gnsa"))
    for g in range(NSA_GROUPS):
        gs_o[0, g] = gs[:, LANES * g:LANES * (g + 1)]

    ctab = ctab_ref[...]
    stab = stab_ref[...]
    scale = (MLA_NOPE + MLA_ROPE) ** -0.5 * LOG2E
    cqn =(_rms(proj("cq")) * qn_ref[...]).astype(BF16)
    q_a = _dot(cqn, wqa_ref[...])
    q_b = _dot(cqn, wqb_ref[...])
    ckvn = (_rms(proj("ckv")) * kvn_ref[...]).astype(BF16)
    kn = _dot(ckvn, wuk_ref[...])
    vv = _dot(ckvn, wuv_ref[...])
    kr = proj("kr")
    krope = kr[:, :LANES] * ctab + kr[:, LANES:] * stab
    for h in range(MLA_HEADS):
        sl = slice(LANES * h, LANES * (h + 1))
        qcat_o[0, :, sl] = ((q_a[:, sl] * ctab + q_b[:, sl] * stab) * scale).astype(BF16)
        kcat_o[0, :, sl] = (kn[:, sl] + krope).astype(BF16)
        vcat_o[0, :, sl] = jnp.where(lane == MLA_V, 1.0, vv[:, sl]).astype(BF16)

    gm_o[0] = _sigmoid(proj("gm"))


def _inproj(x, sh, sc, w_all, wqa, wqb, wuk, wuv, qn, kvn, ctab, stab, *, tm=512):
    B, S, D = x.shape
    G = NSA_GROUPS
    grid = (B, S // tm)
    vec = pl.BlockSpec((1, 1, D), lambda b, i: (b, 0, 0))

    def res(a):
        return _resident(a.shape, lambda b, i: (0,) * a.ndim)

    row = lambda w: pl.BlockSpec((1, tm, w), lambda b, i: (b, i, 0))
    grp = lambda w: pl.BlockSpec((1, G, tm, w), lambda b, i: (b, 0, i, 0))
    tab = pl.BlockSpec((tm, LANES), lambda b, i: (i, 0))
    out_shape = (
        jax.ShapeDtypeStruct((B, S, 512), BF16),
        jax.ShapeDtypeStruct((B, S, 256), F32),
        jax.ShapeDtypeStruct((B, G, S, 256), BF16),
        jax.ShapeDtypeStruct((B, G, S, 256), BF16),
        jax.ShapeDtypeStruct((B, G, S, 256), BF16),
        jax.ShapeDtypeStruct((B, G, S, 256), BF16),
        jax.ShapeDtypeStruct((B, G, S, 128), F32),
        jax.ShapeDtypeStruct((B, S, MLA_HEADS * LANES), BF16),
        jax.ShapeDtypeStruct((B, S, MLA_HEADS * LANES), BF16),
        jax.ShapeDtypeStruct((B, S, MLA_HEADS * LANES), BF16),
        jax.ShapeDtypeStruct((B, S, 2048), F32),
    )
    out_specs = (row(512), row(256), grp(256), grp(256), grp(256), grp(256), grp(128),
                 row(MLA_HEADS * LANES), row(MLA_HEADS * LANES), row(MLA_HEADS * LANES), row(2048))
    return pl.pallas_call(
        functools.partial(_inproj_kernel, tm=tm),
        grid=grid,
        in_specs=[pl.BlockSpec((1, tm, D), lambda b, i: (b, i, 0)), vec, vec,
                  res(w_all), res(wqa), res(wqb), res(wuk), res(wuv), res(qn), res(kvn), tab, tab],
        out_specs=out_specs,
        out_shape=out_shape,
        compiler_params=_params("parallel", "parallel"),
        name="in_proj",
    )(x, sh, sc, w_all, wqa, wqb, wuk, wuv, qn, kvn, ctab, stab)


def _compress_kernel(a_ref, pea_ref, peb_ref, wa_ref, wb_ref, w2_ref, kcat_o, vct_o):
    a = a_ref[0]
    n = a.shape[0]
    pa = _dot((a + pea_ref[...]).astype(BF16), wa_ref[...])
    pb = _dot((a + peb_ref[...]).astype(BF16), wb_ref[...])
    h = pa + pltpu.roll(pb, n - 1, 0)
    h = 0.5 * h * (1.0 + jnp.tanh(0.7978845608028654 * (h + 0.044715 * (h * h * h))))
    out = _dot(h.astype(BF16), w2_ref[...])
    out_t = out.T
    lane = lax.broadcasted_iota(jnp.int32, (n, LANES), 1)
    row = lax.broadcasted_iota(jnp.int32, (n, LANES), 0)
    pair = jnp.right_shift(row, 1)
    within = (row - 2 * pair) * CMP_STRIDE + (CMP_BLOCK - 1)
    kextra = jnp.where(lane < NSA_D + N_PIECES, pair.astype(F32),
                       jnp.where(lane < NSA_D + 2 * N_PIECES, within.astype(F32), 0.0))
    kk = out[:, 0:LANES] * (NSA_D ** -0.5 * LOG2E)
    for g in range(NSA_GROUPS):
        x = kk if g == 0 else pltpu.roll(kk, NSA_D, 1)
        lo = x - x.astype(BF16).astype(F32)
        kcat_o[0, g, :, 0:LANES] = jnp.where(lane < NSA_D, x, kextra).astype(BF16)
        kcat_o[0, g, :, LANES:2 * LANES] = jnp.where(lane < NSA_D, lo, 0.0).astype(BF16)
        vct_o[0, g] = out_t[2 * NSA_D + NSA_D * g:2 * NSA_D + NSA_D * (g + 1), :].astype(BF16)


def _compress(kcvc, pea, peb, wa, wb, w2):
    B, S, W = kcvc.shape
    n = S // CMP_STRIDE
    a = kcvc.reshape(B, n, CMP_STRIDE * W)
    G = NSA_GROUPS
    full = lambda arr: _resident(arr.shape, lambda b: (0,) * arr.ndim)
    return pl.pallas_call(
        _compress_kernel,
        grid=(B,),
        in_specs=[pl.BlockSpec((1, n, CMP_STRIDE * W), lambda b: (b, 0, 0)),
                  full(pea), full(peb), full(wa), full(wb), full(w2)],
        out_specs=(pl.BlockSpec((1, G, n, 2 * LANES), lambda b: (b, 0, 0, 0)),
                   pl.BlockSpec((1, G, NSA_D, n), lambda b: (b, 0, 0, 0))),
        out_shape=(jax.ShapeDtypeStruct((B, G, n, 2 * LANES), BF16),
                   jax.ShapeDtypeStruct((B, G, NSA_D, n), BF16)),
        compiler_params=_params("parallel"),
        name="compress",
    )(a, pea, peb, wa, wb, w2)


def _nsa_select_kernel(q_ref, kcat_ref, vct_ref, oc_o, qaug_o, qc_scr, s_scr, p_scr, *, n_top):
    g = pl.program_id(1)
    i = pl.program_id(2)
    tq = q_ref.shape[1]
    ncp = kcat_ref.shape[2]
    q0 = i * tq
    scale = NSA_D ** -0.5

    lane = lax.broadcasted_iota(jnp.int32, (tq, LANES), 1)
    qf = q_ref[0].astype(F32)
    for h in range(NSA_HPG):
        rs = slice(tq * h, tq * (h + 1))
        x = qf[:, LANES * (h // 2):LANES * (h // 2 + 1)]
        if h % 2 == 1:
            x = pltpu.roll(x, NSA_D, 1)
        extra = jnp.zeros((tq, LANES), F32)
        extra_c = jnp.zeros((tq, LANES), F32)
        for k in range(N_PIECES):
            piece = jnp.where(g == 0, SLOPE_PIECES[h][k], SLOPE_PIECES[NSA_HPG + h][k])
            extra = jnp.where(lane == NSA_D + k, piece,
                              jnp.where(lane == NSA_D + N_PIECES + k, piece * SLC_BLOCK, extra))
            extra_c = jnp.where(lane == NSA_D + k, piece * (2 * CMP_STRIDE),
                                jnp.where(lane == NSA_D + N_PIECES + k, piece, extra_c))
        qaug_o[0, 0, 0, rs, LANES:2 * LANES] = jnp.where(lane < NSA_D, x * (scale * LOG2E), extra).astype(BF16)
        qc_scr[rs, 0:LANES] = jnp.where(lane < NSA_D, x, extra_c).astype(BF16)
        qc_scr[rs, LANES:2 * LANES] = jnp.where(lane < NSA_D, x, 0.0).astype(BF16)

    s_scr[...] = _dot_nt(kcat_ref[0, 0], qc_scr[...])
    cmp_end = lax.broadcasted_iota(jnp.int32, (ncp, tq), 0) * CMP_STRIDE + (CMP_BLOCK - 1)
    t_c = q0 + lax.broadcasted_iota(jnp.int32, (ncp, tq), 1)
    mask_c = cmp_end <= t_c
    psum = jnp.zeros((ncp, tq), F32)
    inv = []
    for h in range(NSA_HPG):
        cs = slice(tq * h, tq * (h + 1))
        s = jnp.where(mask_c, s_scr[:, cs], -jnp.inf)
        m = jnp.max(s, axis=0, keepdims=True)
        m = jnp.where(m == -jnp.inf, 0.0, m)
        p = jnp.exp2(s - m)
        r = 1.0 / jnp.maximum(jnp.sum(p, axis=0, keepdims=True), 1.0)
        p_scr[:, cs] = p.astype(BF16)
        psum = psum + p * r
        inv.append(r)
    oc_t = _dot(vct_ref[0, 0], p_scr[...])
    oc_o[0] = jnp.concatenate([oc_t[:, tq * h:tq * (h + 1)] * inv[h] for h in range(NSA_HPG)], axis=0).T

    ratio = SLC_BLOCK // CMP_STRIDE
    r = CMP_BLOCK // CMP_STRIDE
    bb = lax.broadcasted_iota(jnp.int32, (LANES, ncp), 0)
    mm = lax.broadcasted_iota(jnp.int32, (LANES, ncp), 1)
    a_t = jnp.where((mm >= ratio * bb - (r - 1)) & (mm <= ratio * bb + ratio - 1), 1.0, 0.0).astype(BF16)
    p_hi = psum.astype(BF16)
    p_lo = (psum - p_hi.astype(F32)).astype(BF16)
    imp = _dot(a_t, p_hi) + _dot(a_t, p_lo)

    blk = lax.broadcasted_iota(jnp.int32, (LANES, tq), 0)
    t_s = q0 + lax.broadcasted_iota(jnp.int32, (LANES, tq), 1)
    cur = jnp.right_shift(t_s, SLC_SHIFT)
    forced = (blk == 0) | (blk == cur) | (blk == cur - 1)
    valid = blk * SLC_BLOCK <= t_s
    cand = jnp.where(forced, jnp.inf, jnp.where(valid, imp, -jnp.inf))
    blkf = blk.astype(F32)

    def pick_one(_, carry):
        cand, sel = carry
        m = jnp.max(cand, axis=0, keepdims=True)
        idx = jnp.min(jnp.where(cand == m, blkf, float(LANES)), axis=0, keepdims=True)
        pick = blkf == idx
        return jnp.where(pick, -jnp.inf, cand), jnp.where(pick, 1.0, sel)

    _, sel = lax.fori_loop(0, n_top, pick_one, (cand, jnp.zeros((LANES, tq), F32)))
    pen_t = jnp.where(valid, jnp.where(sel > 0.0, 0.0, NEG), NEG)
    pen = pen_t.T.astype(BF16)
    for h in range(NSA_HPG):
        qaug_o[0, 0, 0, tq * h:tq * (h + 1), 0:LANES] = pen


def _nsa_select(qa, kc, vct, *, n_top):
    B, S, _ = qa.shape
    G = NSA_GROUPS
    tq = Q_BLOCK
    nq = S // tq
    ncp = kc.shape[2]
    gw = NSA_HPG * NSA_D
    return pl.pallas_call(
        functools.partial(_nsa_select_kernel, n_top=n_top),
        grid=(B, G, nq),
        in_specs=[pl.BlockSpec((1, tq, gw), lambda b, g, i: (b, i, g)),
                  pl.BlockSpec((1, 1, ncp, 2 * LANES), lambda b, g, i: (b, g, 0, 0)),
                  pl.BlockSpec((1, 1, NSA_D, ncp), lambda b, g, i: (b, g, 0, 0))],
        out_specs=(pl.BlockSpec((1, tq, gw), lambda b, g, i: (b, i, g)),
                   pl.BlockSpec((1, 1, 1, NSA_HPG * tq, 2 * LANES), lambda b, g, i: (b, g, i, 0, 0))),
        out_shape=(jax.ShapeDtypeStruct((B, S, G * gw), F32),
                   jax.ShapeDtypeStruct((B, G, nq, NSA_HPG * tq, 2 * LANES), BF16)),
        scratch_shapes=[pltpu.VMEM((NSA_HPG * tq, 2 * LANES), BF16),
                        pltpu.VMEM((ncp, NSA_HPG * tq), F32), pltpu.VMEM((ncp, NSA_HPG * tq), BF16)],
        compiler_params=_params("parallel", "parallel", "parallel"),
        name="nsa_select",
    )(qa, kc, vct)


CHUNK = 64


def _scores(q, k, s_scr):
    s_scr[:, :k.shape[0]] = _dot_nt(q, k)


def _softmax_pv(s_scr, p_scr, v, m_scr, acc_scr, bias_fn=None, bias_key=None):
    rows, n = m_scr.shape[0], v.shape[0]
    biases = {}
    for r0 in range(0, rows, CHUNK):
        rs = slice(r0, r0 + CHUNK)
        s = s_scr[rs, :n]
        if bias_fn is not None:
            key = r0 if bias_key is None else bias_key(r0)
            if key not in biases:
                biases[key] = bias_fn(r0)
            s = s + biases[key]
        m_old = m_scr[rs]
        m_new = jnp.maximum(m_old, jnp.broadcast_to(jnp.max(s, axis=-1, keepdims=True), m_old.shape))
        m_scr[rs] = m_new
        alpha = jnp.exp2(m_old - m_new)
        for c in range(0, acc_scr.shape[1], LANES):
            acc_scr[rs, c:c + LANES] = acc_scr[rs, c:c + LANES] * alpha
        for c in range(0, n, LANES):
            p_scr[rs, c:c + LANES] = jnp.exp2((s[:, c:c + LANES] - m_new).astype(BF16))
    acc_scr[...] += _dot(p_scr[:, :n], v)


def _causal_flash(q, k_ref, v_ref, n_full, tk, s_a, s_b, p_a, p_b, m_scr, acc_scr, tail_bias, bias_key=None,
                  overlap_fn=None):
    k_tile = lambda j: k_ref[pl.ds(pl.multiple_of(j * tk, tk), tk), :]
    v_tile = lambda j: v_ref[pl.ds(pl.multiple_of(j * tk, tk), tk), :]
    n_pairs = lax.div(n_full, 2)

    _scores(q, k_tile(0), s_a)
    if overlap_fn is not None:
        overlap_fn()

    def pair(jj, carry):
        j = 2 * jj
        _scores(q, k_tile(j + 1), s_b)
        _softmax_pv(s_a, p_a, v_tile(j), m_scr, acc_scr)
        _scores(q, k_tile(j + 2), s_a)
        _softmax_pv(s_b, p_b, v_tile(j + 1), m_scr, acc_scr)
        return carry

    lax.fori_loop(0, n_pairs, pair, 0)
    j = 2 * n_pairs

    @pl.when(j < n_full)
    def _():
        _scores(q, k_tile(j + 1), s_b)
        _softmax_pv(s_a, p_a, v_tile(j), m_scr, acc_scr)
        _softmax_pv(s_b, p_b, v_tile(j + 1), m_scr, acc_scr, bias_fn=tail_bias, bias_key=bias_key)

    @pl.when(j == n_full)
    def _():
        _softmax_pv(s_a, p_a, v_tile(j), m_scr, acc_scr, bias_fn=tail_bias, bias_key=bias_key)


def _causal_bias(t0, pos0, n):
    t = t0 + lax.broadcasted_iota(jnp.int32, (CHUNK, n), 0)
    pos = pos0 + lax.broadcasted_iota(jnp.int32, (CHUNK, n), 1)
    return jnp.where(pos <= t, 0.0, NEG)


def _nsa_attend_kernel(qaug_ref, ks_ref, vs_ref, kw_ref, vw_ref, oc_ref, gs_ref, o_ref,
                       s_scr, s_b, s_w, p_scr, p_b, p_w, m_scr, mw_scr, accs_scr, accw_scr, *, tk):
    i = pl.program_id(2)
    qb = qaug_ref.shape[2]
    blk_rows = NSA_HPG * Q_BLOCK
    rows = qb * blk_rows
    q = qaug_ref[0, 0].reshape(rows, 2 * LANES)
    q0 = i * (qb * Q_BLOCK)
    row_dt = lambda r0: (r0 // blk_rows) * Q_BLOCK + r0 % Q_BLOCK

    nw = WINDOW + qb * Q_BLOCK
    offw = pl.multiple_of(jnp.maximum(q0 - WINDOW, 0), Q_BLOCK)

    def window_bias(r0):
        d = (q0 + row_dt(r0) + lax.broadcasted_iota(jnp.int32, (CHUNK, nw), 0)
             - (offw + lax.broadcasted_iota(jnp.int32, (CHUNK, nw), 1)))
        return jnp.where(d >= 0, jnp.where(d < WINDOW, 0.0, NEG), NEG)

    mw_scr[...] = jnp.full((rows, LANES), NEG, F32)
    accw_scr[...] = jnp.zeros(accw_scr.shape, F32)
    _scores(q, kw_ref[0, 0, pl.ds(offw, nw), :], s_w)

    def window_softmax():
        _softmax_pv(s_w, p_w, vw_ref[0, 0, pl.ds(offw, nw), :], mw_scr, accw_scr, bias_fn=window_bias,
                    bias_key=row_dt)

    m_scr[...] = jnp.full((rows, LANES), NEG, F32)
    accs_scr[...] = jnp.zeros(accs_scr.shape, F32)
    n_full = lax.div(q0, tk)
    _causal_flash(q, ks_ref.at[0, 0], vs_ref.at[0, 0], n_full, tk, s_scr, s_b, p_scr, p_b, m_scr, accs_scr,
                  tail_bias=lambda r0: _causal_bias(q0 + row_dt(r0), n_full * tk, tk), bias_key=row_dt,
                  overlap_fn=window_softmax)

    gw = NSA_HPG * NSA_D
    gs = gs_ref[0, 0]
    src = lax.broadcasted_iota(jnp.int32, (LANES, 3 * gw), 0)
    dst = lax.broadcasted_iota(jnp.int32, (LANES, 3 * gw), 1)
    branch = jnp.right_shift(dst, gw.bit_length() - 1)
    head = jnp.right_shift(dst & (gw - 1), NSA_D.bit_length() - 1)
    spread = jnp.where(src == 3 * head + branch, 1.0, 0.0).astype(BF16)
    gs_hi = gs.astype(BF16)
    gs_lo = (gs - gs_hi.astype(F32)).astype(BF16)
    g_exp = _dot(gs_hi, spread) + _dot(gs_lo, spread)

    lane = lax.broadcasted_iota(jnp.int32, (Q_BLOCK, LANES), 1)

    def normalised(acc_ref, c, h2):
        r0 = blk_rows * c + Q_BLOCK * 2 * h2
        even = acc_ref[r0:r0 + Q_BLOCK, :]
        odd = acc_ref[r0 + Q_BLOCK:r0 + 2 * Q_BLOCK, :]
        return jnp.where(lane < NSA_D, even[:, :LANES] / even[:, LANES:], odd[:, LANES:] / odd[:, :LANES])

    for c in range(qb):
        qs = slice(Q_BLOCK * c, Q_BLOCK * (c + 1))
        for h2 in range(NSA_HPG // 2):
            ls = slice(LANES * h2, LANES * (h2 + 1))
            gate = lambda j: g_exp[qs, gw * j + LANES * h2:gw * j + LANES * (h2 + 1)]
            o = (gate(0) * oc_ref[0, qs, ls] + gate(1) * normalised(accs_scr, c, h2)
                 + gate(2) * normalised(accw_scr, c, h2))
            o_ref[0, qs, ls] = o.astype(BF16)


def _nsa_attend(qaug, ks, vs, kw, vw, oc, gs, *, tk=512, qb=2):
    B, G, nq = qaug.shape[:3]
    S = ks.shape[2]
    tq = qb * Q_BLOCK
    tk = min(tk, S)
    gw = NSA_HPG * NSA_D
    rows = NSA_HPG * tq
    nw = WINDOW + tq
    assert S >= nw and nq % qb == 0 and tk % tq == 0
    kv = lambda w: pl.BlockSpec((1, 1, S, w), lambda b, g, i: (b, g, 0, 0))
    return pl.pallas_call(
        functools.partial(_nsa_attend_kernel, tk=tk),
        grid=(B, G, nq // qb),
        in_specs=[pl.BlockSpec((1, 1, qb, NSA_HPG * Q_BLOCK, 2 * LANES), lambda b, g, i: (b, g, i, 0, 0)),
                  kv(2 * LANES), kv(2 * LANES), kv(2 * LANES), kv(2 * LANES),
                  pl.BlockSpec((1, tq, gw), lambda b, g, i: (b, i, g)),
                  pl.BlockSpec((1, 1, tq, LANES), lambda b, g, i: (b, g, i, 0))],
        out_specs=pl.BlockSpec((1, tq, gw), lambda b, g, i: (b, i, g)),
        out_shape=jax.ShapeDtypeStruct((B, S, G * gw), BF16),
        scratch_shapes=[pltpu.VMEM((rows, tk), F32), pltpu.VMEM((rows, tk), F32), pltpu.VMEM((rows, nw), F32),
                        pltpu.VMEM((rows, tk), BF16), pltpu.VMEM((rows, tk), BF16), pltpu.VMEM((rows, nw), BF16),
                        pltpu.VMEM((rows, LANES), F32), pltpu.VMEM((rows, LANES), F32),
                        pltpu.VMEM((rows, 2 * LANES), F32), pltpu.VMEM((rows, 2 * LANES), F32)],
        compiler_params=_params("parallel", "parallel", "arbitrary"),
        name="nsa_attend",
    )(qaug, ks, vs, kw, vw, oc, gs)


def _mla_kernel(q_ref, k_ref, v_ref, o_ref, s_a, s_b, p_a, p_b, m_scr, acc_scr, *, t):
    i = pl.program_id(2)
    q = q_ref[0]
    m_scr[...] = jnp.full((t, LANES), NEG, F32)
    acc_scr[...] = jnp.zeros((t, LANES), F32)
    _causal_flash(q, k_ref.at[0], v_ref.at[0], i, t, s_a, s_b, p_a, p_b, m_scr, acc_scr,
                  tail_bias=lambda r0: _causal_bias(r0, 0, t))
    acc = acc_scr[...]
    o_ref[0] = (acc / acc[:, MLA_V:MLA_V + 1]).astype(BF16)


def _mla_attend(qcat, kcat, vcat, *, t=512):
    B, S, _ = qcat.shape
    t = min(t, S)
    H = MLA_HEADS
    kv = pl.BlockSpec((1, S, LANES), lambda b, h, i: (b, 0, h))
    return pl.pallas_call(
        functools.partial(_mla_kernel, t=t),
        grid=(B, H, S // t),
        in_specs=[pl.BlockSpec((1, t, LANES), lambda b, h, i: (b, i, h)), kv, kv],
        out_specs=pl.BlockSpec((1, t, LANES), lambda b, h, i: (b, i, h)),
        out_shape=jax.ShapeDtypeStruct((B, S, H * LANES), BF16),
        scratch_shapes=[pltpu.VMEM((t, t), F32), pltpu.VMEM((t, t), F32),
                        pltpu.VMEM((t, t), BF16), pltpu.VMEM((t, t), BF16),
                        pltpu.VMEM((t, LANES), F32), pltpu.VMEM((t, LANES), F32)],
        compiler_params=_params("parallel", "parallel", "arbitrary"),
        name="mla_attend",
    )(qcat, kcat, vcat)


def _merge_kernel(x_ref, g_ref, gm_ref, oa_ref, ob_ref, wa_ref, wb_ref, wo_ref, o_ref):
    D = x_ref.shape[2]
    gm = gm_ref[0]
    y = gm[:, :D] * _dot(oa_ref[0], wa_ref[...]) + gm[:, D:] * _dot(ob_ref[0], wb_ref[...])
    o_ref[0] = x_ref[0] + g_ref[0] * _dot(y.astype(BF16), wo_ref[...])


def _merge(x, g, gm, oa, ob, wa, wb, wo, *, tm=512):
    B, S, D = x.shape
    row = lambda w: pl.BlockSpec((1, tm, w), lambda b, i: (b, i, 0))
    res = lambda a: _resident(a.shape, lambda b, i: (0, 0))
    return pl.pallas_call(
        _merge_kernel,
        grid=(B, S // tm),
        in_specs=[row(D), pl.BlockSpec((1, 1, D), lambda b, i: (b, 0, 0)), row(2 * D),
                  row(oa.shape[2]), row(ob.shape[2]), res(wa), res(wb), res(wo)],
        out_specs=row(D),
        out_shape=jax.ShapeDtypeStruct((B, S, D), F32),
        compiler_params=_params("parallel", "parallel"),
        name="merge",
    )(x, g, gm, oa, ob, wa, wb, wo)


def _pad_heads(w, n_heads, width):
    k = w.shape[0]
    w = w.reshape(k, n_heads, width)
    return jnp.pad(w, ((0, 0), (0, 0), (0, LANES - width))).reshape(k, n_heads * LANES)


def _rot_cols(w):
    half = w.shape[-1] // 2
    return jnp.concatenate([-w[..., half:], w[..., :half]], axis=-1)


def _prep_inproj_weights(w_in, w_uq, w_uk, w_uv):
    D = w_in.shape[0]
    o = 0
    seg = {}
    for name, width in (("qa", 512), ("kc", 128), ("vc", 128), ("ks", 128), ("vs", 128), ("kw", 128),
                        ("vw", 128), ("gnsa", 3 * NSA_HEADS), ("cq", MLA_Q_RANK), ("ckv", MLA_KV_RANK),
                        ("kr", MLA_ROPE), ("gm", 2048)):
        seg[name] = w_in[:, o:o + width]
        o += width
    per_g = 3 * NSA_HPG
    gn = jnp.concatenate(
        [jnp.pad(seg["gnsa"][:, per_g * g:per_g * (g + 1)], ((0, 0), (0, LANES - per_g)))
         for g in range(NSA_GROUPS)], axis=1)
    z64 = jnp.zeros((D, MLA_NOPE), F32)
    z32 = jnp.zeros((D, LANES - MLA_NOPE - MLA_ROPE), F32)
    kr = jnp.concatenate([z64, seg["kr"], z32, z64, _rot_cols(seg["kr"]), z32], axis=1)
    w_all = jnp.concatenate([seg["qa"], seg["kc"], seg["vc"], seg["ks"], seg["vs"], seg["kw"], seg["vw"],
                             gn, seg["cq"], seg["ckv"], kr, seg["gm"]], axis=1).astype(BF16)
    assert w_all.shape[1] == _SEG_TOTAL
    hd = MLA_NOPE + MLA_ROPE
    uq = w_uq.reshape(-1, MLA_HEADS, hd)
    uq_rot = jnp.concatenate([jnp.zeros_like(uq[..., :MLA_NOPE]), _rot_cols(uq[..., MLA_NOPE:])], axis=-1)
    wqa = _pad_heads(w_uq, MLA_HEADS, hd).astype(BF16)
    wqb = _pad_heads(uq_rot.reshape(-1, MLA_HEADS * hd), MLA_HEADS, hd).astype(BF16)
    wuk = _pad_heads(w_uk, MLA_HEADS, MLA_NOPE).astype(BF16)
    wuv = _pad_heads(w_uv, MLA_HEADS, MLA_V).astype(BF16)
    return w_all, wqa, wqb, wuk, wuv


def _prep_compress_weights(k_w1, k_w2, k_pe, v_w1, v_w2, v_pe):
    d = NSA_D
    half = CMP_STRIDE

    def blockdiag(mats):
        n = len(mats)
        z = jnp.zeros_like(mats[0])
        rows = [jnp.concatenate([mats[r] if c == r else z for c in range(n)], axis=-1) for r in range(n)]
        return jnp.concatenate(rows, axis=-2)

    def first(w1, lo):
        return w1.reshape(CMP_BLOCK, d, d)[lo:lo + half]

    def w1_half(lo):
        k, v = first(k_w1, lo), first(v_w1, lo)
        return blockdiag([k, k, v, v]).reshape(half * 4 * d, 4 * d).astype(BF16)

    def pe_half(lo):
        k, v = k_pe[lo:lo + half], v_pe[lo:lo + half]
        return jnp.concatenate([k, k, v, v], axis=-1).reshape(1, half * 4 * d)

    w2 = blockdiag([k_w2, k_w2, v_w2, v_w2]).astype(BF16)
    return pe_half(0), pe_half(half), w1_half(0), w1_half(half), w2


def _rope_tables(S):
    half = MLA_ROPE // 2
    inv_freq = ROPE_THETA ** (-jnp.arange(half, dtype=F32) / half)
    ang = jnp.arange(S, dtype=F32)[:, None] * inv_freq[None, :]
    cos, sin = jnp.cos(ang), jnp.sin(ang)
    ones = jnp.ones((S, MLA_NOPE), F32)
    z_n = jnp.zeros((S, MLA_NOPE), F32)
    z_p = jnp.zeros((S, LANES - MLA_NOPE - MLA_ROPE), F32)
    ctab = jnp.concatenate([ones, cos, cos, z_p], axis=1)
    stab = jnp.concatenate([z_n, sin, sin, z_p], axis=1)
    return ctab, stab


def kernel(x, c, w_ada, b_ada, ffn1_gate, ffn1_up, ffn1_down, ffn2_gate, ffn2_up, ffn2_down, w_in, cmpk_w1, cmpk_w2, cmpk_pe, cmpv_w1, cmpv_w2, cmpv_pe, mla_q_norm, mla_w_uq, mla_kv_norm, mla_w_uk, mla_w_uv, w_branch_a, w_branch_b, w_out, final_norm):
    B, S, D = x.shape
    L = w_ada.shape[0]
    assert NSA_GROUPS == 2 and S // SLC_BLOCK <= LANES and S // (2 * CMP_STRIDE) <= 256 and S % 512 == 0
    n_top = min(N_SELECT, S // SLC_BLOCK)
    ctab, stab = _rope_tables(S)
    mod = _adaln(c, w_ada, b_ada).reshape(L, B, N_MOD, 1, D)
    bf = lambda w: w.astype(BF16)
    for l in range(L):
        sh1, sc1, g1, sh2, sc2, g2, sh3, sc3, g3 = (mod[l, :, k] for k in range(N_MOD))
        x = _ffn(x, sh1, sc1, g1, bf(ffn1_gate[l]), bf(ffn1_up[l]), bf(ffn1_down[l]), final_norm, final=False)

        w_all, wqa, wqb, wuk, wuv = _prep_inproj_weights(w_in[l], mla_w_uq[l], mla_w_uk[l], mla_w_uv[l])
        (qa, kcvc, ks, vs, kw, vw, gs, qcat, kcat, vcat, gm) = _inproj(
            x, sh2, sc2, w_all, wqa, wqb, wuk, wuv,
            mla_q_norm[l].reshape(1, -1), mla_kv_norm[l].reshape(1, -1), ctab, stab)
        kc, vct = _compress(kcvc, *_prep_compress_weights(
            cmpk_w1[l], cmpk_w2[l], cmpk_pe[l], cmpv_w1[l], cmpv_w2[l], cmpv_pe[l]))
        oc, qaug = _nsa_select(qa, kc, vct, n_top=n_top)
        oa = _nsa_attend(qaug, ks, vs, kw, vw, oc, gs)
        ob = _mla_attend(qcat, kcat, vcat)
        wb_pad = jnp.pad(w_branch_b[l].reshape(MLA_HEADS, MLA_V, D),
                         ((0, 0), (0, LANES - MLA_V), (0, 0))).reshape(MLA_HEADS * LANES, D)
        x = _merge(x, g2, gm, oa, ob, bf(w_branch_a[l]), bf(wb_pad), bf(w_out[l]))

        x = _ffn(x, sh3, sc3, g3, bf(ffn2_gate[l]), bf(ffn2_up[l]), bf(ffn2_down[l]), final_norm,
                 final=(l == L - 1))
    return x
```

```python
import functools
import math

import ml_dtypes
import numpy as np
import jax
import jax.numpy as jnp
from jax import lax
from jax.experimental import pallas as pl
from jax.experimental.pallas import tpu as pltpu

F32 = jnp.float32
BF16 = jnp.bfloat16

NSA_HEADS = 8
NSA_GROUPS = 2
NSA_HPG = NSA_HEADS // NSA_GROUPS
NSA_D = 64
CMP_BLOCK = 32
CMP_STRIDE = 16
SLC_BLOCK = 64
N_SELECT = 16
WINDOW = 512
MLA_HEADS = 8
MLA_NOPE = 64
MLA_ROPE = 32
MLA_V = 64
MLA_Q_RANK = 384
MLA_KV_RANK = 256
ROPE_THETA = 10000.0
N_MOD = 9
Q_BLOCK = 128
NORM_EPS = 1e-6

LANES = 128
MXU_WIDTH = 256
SLC_SHIFT = SLC_BLOCK.bit_length() - 1
assert 1 << SLC_SHIFT == SLC_BLOCK
LOG2E = math.log2(math.e)
SLOPES = [2.0 ** (-8.0 * (k + 1) / NSA_HEADS) for k in range(NSA_HEADS)]
N_PIECES = 3


def _bf16_pieces(v):
    out = []
    for _ in range(N_PIECES):
        p = float(np.float32(v).astype(ml_dtypes.bfloat16))
        out.append(p)
        v -= p
    return out


SLOPE_PIECES = [_bf16_pieces(v * LOG2E) for v in SLOPES]
NEG = -1e30
VMEM_LIMIT = 56 * 1024 * 1024


def _rms(xf):
    return xf * lax.rsqrt(jnp.mean(xf * xf, -1, keepdims=True) + NORM_EPS)


def _sigmoid(x):
    return 1.0 / (1.0 + jnp.exp(-x))


def _dot(a, b):
    return jnp.dot(a, b, preferred_element_type=F32)


def _dot_nt(a, b, precision=None):
    return lax.dot_general(a, b, (((1,), (1,)), ((), ())), precision=precision,
                           preferred_element_type=F32)


def _params(*sem):
    return pltpu.CompilerParams(dimension_semantics=sem, vmem_limit_bytes=VMEM_LIMIT)


def _resident(shape, index_map):
    return pl.BlockSpec(shape, index_map, pipeline_mode=pl.Buffered(1))


def _adaln_kernel(c_ref, w_ref, b_ref, o_ref):
    c = c_ref[...]
    a = c * _sigmoid(c)
    o_ref[0] = jnp.dot(a, w_ref[0], precision=lax.Precision.HIGHEST,
                       preferred_element_type=F32) + b_ref[0]


def _adaln(c, w_ada, b_ada):
    L, D, N = w_ada.shape
    B = c.shape[0]
    rows = 8
    c_pad = jnp.zeros((rows, D), F32).at[:B].set(c)
    tn = N // 8
    out = pl.pallas_call(
        _adaln_kernel,
        grid=(L, N // tn),
        in_specs=[pl.BlockSpec((rows, D), lambda l, j: (0, 0)),
                  pl.BlockSpec((1, D, tn), lambda l, j: (l, 0, j)),
                  pl.BlockSpec((1, 1, tn), lambda l, j: (l, 0, j))],
        out_specs=pl.BlockSpec((1, rows, tn), lambda l, j: (l, 0, j)),
        out_shape=jax.ShapeDtypeStruct((L, rows, N), F32),
        compiler_params=_params("parallel", "parallel"),
        name="adaln_mod",
    )(c_pad, w_ada, b_ada.reshape(L, 1, N))
    return out[:, :B]


def _ffn_kernel(x_ref, sh_ref, sc_ref, g_ref, wg_ref, wu_ref, wd_ref, fn_ref, o_ref, *, splits, final):
    x = x_ref[0]
    u = (_rms(x) * (1.0 + sc_ref[0]) + sh_ref[0]).astype(BF16)
    acc = jnp.zeros(x.shape, F32)
    for f0, f1 in zip(splits[:-1], splits[1:]):
        hg = _dot(u, wg_ref[:, f0:f1])
        hu = _dot(u, wu_ref[:, f0:f1])
        a = (hg * _sigmoid(hg) * hu).astype(BF16)
        acc = acc + _dot(a, wd_ref[f0:f1, :])
    y = x + (0.5 * g_ref[0]) * acc
    if final:
        y = _rms(y) * fn_ref[...]
    o_ref[0] = y


def _ffn(x, sh, sc, g, wg, wu, wd, final_norm, *, final, tm=512):
    B, S, D = x.shape
    n_ff = wg.shape[1]
    cut = (n_ff // 2 + MXU_WIDTH - 1) // MXU_WIDTH * MXU_WIDTH
    splits = (0, cut, n_ff) if 0 < cut < n_ff else (0, n_ff)
    vec = pl.BlockSpec((1, 1, D), lambda b, i: (b, 0, 0))
    return pl.pallas_call(
        functools.partial(_ffn_kernel, splits=splits, final=final),
        grid=(B, S // tm),
        in_specs=[pl.BlockSpec((1, tm, D), lambda b, i: (b, i, 0)), vec, vec, vec,
                  _resident((D, n_ff), lambda b, i: (0, 0)),
                  _resident((D, n_ff), lambda b, i: (0, 0)),
                  _resident((n_ff, D), lambda b, i: (0, 0)),
                  pl.BlockSpec((1, D), lambda b, i: (0, 0))],
        out_specs=pl.BlockSpec((1, tm, D), lambda b, i: (b, i, 0)),
        out_shape=jax.ShapeDtypeStruct((B, S, D), F32),
        compiler_params=_params("parallel", "parallel"),
        name="ffn_final" if final else "ffn",
    )(x, sh, sc, g, wg, wu, wd, final_norm.reshape(1, D))


_SEG = {}
_o = 0
for _name, _w in (("qa", 512), ("kcvc", 256), ("kv4", 512), ("gnsa", 256), ("cq", MLA_Q_RANK),
                  ("ckv", MLA_KV_RANK), ("kr", 256), ("gm", 2048)):
    _SEG[_name] = (_o, _o + _w)
    _o += _w
_SEG_TOTAL = _o


def _inproj_kernel(x_ref, sh_ref, sc_ref, w_ref, wqa_ref, wqb_ref, wuk_ref, wuv_ref, qn_ref, kvn_ref,
                   ctab_ref, stab_ref,
                   qa_o, kcvc_o, ks_o, vs_o, kw_o, vw_o, gs_o, qcat_o, kcat_o, vcat_o, gm_o, *, tm):
    i = pl.program_id(1)
    x = x_ref[0]
    u = (_rms(x) * (1.0 + sc_ref[0]) + sh_ref[0]).astype(BF16)

    def proj(name):
        a, b = _SEG[name]
        return _dot(u, w_ref[:, a:b])

    qa_o[0] = proj("qa").astype(BF16)
    kcvc_o[0] = proj("kcvc")

    kv = proj("kv4")
    lane = lax.broadcasted_iota(jnp.int32, (tm, LANES), 1)
    pos = i * tm + lax.broadcasted_iota(jnp.int32, (tm, LANES), 0)
    blk = jnp.right_shift(pos, SLC_SHIFT)
    rem = pos - blk * SLC_BLOCK
    onehot = jnp.where(lane == blk, 1.0, 0.0).astype(BF16)
    zeros = jnp.zeros((tm, LANES), BF16)
    kextra = jnp.where(lane < NSA_D + N_PIECES, rem.astype(F32),
                       jnp.where(lane < NSA_D + 2 * N_PIECES, blk.astype(F32), 0.0))
    for g in range(NSA_GROUPS):
        for src, k_o, v_o, first in ((0, ks_o, vs_o, onehot), (2, kw_o, vw_o, zeros)):
            kk = kv[:, LANES * src:LANES * (src + 1)]
            vv = kv[:, LANES * (src + 1):LANES * (src + 2)]
            vr = pltpu.roll(vv, NSA_D, 1)
            v_lo, v_hi = (vv, vr) if g == 0 else (vr, vv)
            if g == 1:
                kk = pltpu.roll(kk, NSA_D, 1)
            k_o[0, g, :, 0:LANES] = first
            k_o[0, g, :, LANES:2 * LANES] = jnp.where(lane < NSA_D, kk, kextra).astype(BF16)
            v_o[0, g, :, 0:LANES] = jnp.where(lane < NSA_D, v_lo, 1.0).astype(BF16)
            v_o[0, g, :, LANES:2 * LANES] = jnp.where(lane < NSA_D, 1.0, v_hi).astype(BF16)

    gs = _sigmoid(proj("gnsa"))
    for g in range(NSA_GROUPS):
        gs_o[0, g] = gs[:, LANES * g:LANES * (g + 1)]

    ctab = ctab_ref[...]
    stab = stab_ref[...]
    scale = (MLA_NOPE + MLA_ROPE) ** -0.5 * LOG2E
    cqn =(_rms(proj("cq")) * qn_ref[...]).astype(BF16)
    q_a = _dot(cqn, wqa_ref[...])
    q_b = _dot(cqn, wqb_ref[...])
    ckvn = (_rms(proj("ckv")) * kvn_ref[...]).astype(BF16)
    kn = _dot(ckvn, wuk_ref[...])
    vv = _dot(ckvn, wuv_ref[...])
    kr = proj("kr")
    krope = kr[:, :LANES] * ctab + kr[:, LANES:] * stab
    for h in range(MLA_HEADS):
        sl = slice(LANES * h, LANES * (h + 1))
        qcat_o[0, :, sl] = ((q_a[:, sl] * ctab + q_b[:, sl] * stab) * scale).astype(BF16)
        kcat_o[0, :, sl] = (kn[:, sl] + krope).astype(BF16)
        vcat_o[0, :, sl] = jnp.where(lane == MLA_V, 1.0, vv[:, sl]).astype(BF16)

    gm_o[0] = _sigmoid(proj("gm"))


def _inproj(x, sh, sc, w_all, wqa, wqb, wuk, wuv, qn, kvn, ctab, stab, *, tm=512):
    B, S, D = x.shape
    G = NSA_GROUPS
    grid = (B, S // tm)
    vec = pl.BlockSpec((1, 1, D), lambda b, i: (b, 0, 0))

    def res(a):
        return _resident(a.shape, lambda b, i: (0,) * a.ndim)

    row = lambda w: pl.BlockSpec((1, tm, w), lambda b, i: (b, i, 0))
    grp = lambda w: pl.BlockSpec((1, G, tm, w), lambda b, i: (b, 0, i, 0))
    tab = pl.BlockSpec((tm, LANES), lambda b, i: (i, 0))
    out_shape = (
        jax.ShapeDtypeStruct((B, S, 512), BF16),
        jax.ShapeDtypeStruct((B, S, 256), F32),
        jax.ShapeDtypeStruct((B, G, S, 256), BF16),
        jax.ShapeDtypeStruct((B, G, S, 256), BF16),
        jax.ShapeDtypeStruct((B, G, S, 256), BF16),
        jax.ShapeDtypeStruct((B, G, S, 256), BF16),
        jax.ShapeDtypeStruct((B, G, S, 128), F32),
        jax.ShapeDtypeStruct((B, S, MLA_HEADS * LANES), BF16),
        jax.ShapeDtypeStruct((B, S, MLA_HEADS * LANES), BF16),
        jax.ShapeDtypeStruct((B, S, MLA_HEADS * LANES), BF16),
        jax.ShapeDtypeStruct((B, S, 2048), F32),
    )
    out_specs = (row(512), row(256), grp(256), grp(256), grp(256), grp(256), grp(128),
                 row(MLA_HEADS * LANES), row(MLA_HEADS * LANES), row(MLA_HEADS * LANES), row(2048))
    return pl.pallas_call(
        functools.partial(_inproj_kernel, tm=tm),
        grid=grid,
        in_specs=[pl.BlockSpec((1, tm, D), lambda b, i: (b, i, 0)), vec, vec,
                  res(w_all), res(wqa), res(wqb), res(wuk), res(wuv), res(qn), res(kvn), tab, tab],
        out_specs=out_specs,
        out_shape=out_shape,
        compiler_params=_params("parallel", "parallel"),
        name="in_proj",
    )(x, sh, sc, w_all, wqa, wqb, wuk, wuv, qn, kvn, ctab, stab)


def _compress_kernel(a_ref, pea_ref, peb_ref, wa_ref, wb_ref, w2_ref, kcat_o, vct_o):
    a = a_ref[0]
    n = a.shape[0]
    pa = _dot((a + pea_ref[...]).astype(BF16), wa_ref[...])
    pb = _dot((a + peb_ref[...]).astype(BF16), wb_ref[...])
    h = pa + pltpu.roll(pb, n - 1, 0)
    h = 0.5 * h * (1.0 + jnp.tanh(0.7978845608028654 * (h + 0.044715 * (h * h * h))))
    out = _dot(h.astype(BF16), w2_ref[...])
    out_t = out.T
    lane = lax.broadcasted_iota(jnp.int32, (n, LANES), 1)
    row = lax.broadcasted_iota(jnp.int32, (n, LANES), 0)
    pair = jnp.right_shift(row, 1)
    within = (row - 2 * pair) * CMP_STRIDE + (CMP_BLOCK - 1)
    kextra = jnp.where(lane < NSA_D + N_PIECES, pair.astype(F32),
                       jnp.where(lane < NSA_D + 2 * N_PIECES, within.astype(F32), 0.0))
    kk = out[:, 0:LANES] * (NSA_D ** -0.5 * LOG2E)
    for g in range(NSA_GROUPS):
        x = kk if g == 0 else pltpu.roll(kk, NSA_D, 1)
        lo = x - x.astype(BF16).astype(F32)
        kcat_o[0, g, :, 0:LANES] = jnp.where(lane < NSA_D, x, kextra).astype(BF16)
        kcat_o[0, g, :, LANES:2 * LANES] = jnp.where(lane < NSA_D, lo, 0.0).astype(BF16)
        vct_o[0, g] = out_t[2 * NSA_D + NSA_D * g:2 * NSA_D + NSA_D * (g + 1), :].astype(BF16)


def _compress(kcvc, pea, peb, wa, wb, w2):
    B, S, W = kcvc.shape
    n = S // CMP_STRIDE
    a = kcvc.reshape(B, n, CMP_STRIDE * W)
    G = NSA_GROUPS
    full = lambda arr: _resident(arr.shape, lambda b: (0,) * arr.ndim)
    return pl.pallas_call(
        _compress_kernel,
        grid=(B,),
        in_specs=[pl.BlockSpec((1, n, CMP_STRIDE * W), lambda b: (b, 0, 0)),
                  full(pea), full(peb), full(wa), full(wb), full(w2)],
        out_specs=(pl.BlockSpec((1, G, n, 2 * LANES), lambda b: (b, 0, 0, 0)),
                   pl.BlockSpec((1, G, NSA_D, n), lambda b: (b, 0, 0, 0))),
        out_shape=(jax.ShapeDtypeStruct((B, G, n, 2 * LANES), BF16),
                   jax.ShapeDtypeStruct((B, G, NSA_D, n), BF16)),
        compiler_params=_params("parallel"),
        name="compress",
    )(a, pea, peb, wa, wb, w2)


def _nsa_select_kernel(q_ref, kcat_ref, vct_ref, oc_o, qaug_o, qc_scr, s_scr, p_scr, *, n_top):
    g = pl.program_id(1)
    i = pl.program_id(2)
    tq = q_ref.shape[1]
    ncp = kcat_ref.shape[2]
    q0 = i * tq
    scale = NSA_D ** -0.5

    lane = lax.broadcasted_iota(jnp.int32, (tq, LANES), 1)
    qf = q_ref[0].astype(F32)
    for h in range(NSA_HPG):
        rs = slice(tq * h, tq * (h + 1))
        x = qf[:, LANES * (h // 2):LANES * (h // 2 + 1)]
        if h % 2 == 1:
            x = pltpu.roll(x, NSA_D, 1)
        extra = jnp.zeros((tq, LANES), F32)
        extra_c = jnp.zeros((tq, LANES), F32)
        for k in range(N_PIECES):
            piece = jnp.where(g == 0, SLOPE_PIECES[h][k], SLOPE_PIECES[NSA_HPG + h][k])
            extra = jnp.where(lane == NSA_D + k, piece,
                              jnp.where(lane == NSA_D + N_PIECES + k, piece * SLC_BLOCK, extra))
            extra_c = jnp.where(lane == NSA_D + k, piece * (2 * CMP_STRIDE),
                                jnp.where(lane == NSA_D + N_PIECES + k, piece, extra_c))
        qaug_o[0, 0, 0, rs, LANES:2 * LANES] = jnp.where(lane < NSA_D, x * (scale * LOG2E), extra).astype(BF16)
        qc_scr[rs, 0:LANES] = jnp.where(lane < NSA_D, x, extra_c).astype(BF16)
        qc_scr[rs, LANES:2 * LANES] = jnp.where(lane < NSA_D, x, 0.0).astype(BF16)

    s_scr[...] = _dot_nt(kcat_ref[0, 0], qc_scr[...])
    cmp_end = lax.broadcasted_iota(jnp.int32, (ncp, tq), 0) * CMP_STRIDE + (CMP_BLOCK - 1)
    t_c = q0 + lax.broadcasted_iota(jnp.int32, (ncp, tq), 1)
    mask_c = cmp_end <= t_c
    psum = jnp.zeros((ncp, tq), F32)
    inv = []
    for h in range(NSA_HPG):
        cs = slice(tq * h, tq * (h + 1))
        s = jnp.where(mask_c, s_scr[:, cs], -jnp.inf)
        m = jnp.max(s, axis=0, keepdims=True)
        m = jnp.where(m == -jnp.inf, 0.0, m)
        p = jnp.exp2(s - m)
        r = 1.0 / jnp.maximum(jnp.sum(p, axis=0, keepdims=True), 1.0)
        p_scr[:, cs] = p.astype(BF16)
        psum = psum + p * r
        inv.append(r)
    oc_t = _dot(vct_ref[0, 0], p_scr[...])
    oc_o[0] = jnp.concatenate([oc_t[:, tq * h:tq * (h + 1)] * inv[h] for h in range(NSA_HPG)], axis=0).T

    ratio = SLC_BLOCK // CMP_STRIDE
    r = CMP_BLOCK // CMP_STRIDE
    bb = lax.broadcasted_iota(jnp.int32, (LANES, ncp), 0)
    mm = lax.broadcasted_iota(jnp.int32, (LANES, ncp), 1)
    a_t = jnp.where((mm >= ratio * bb - (r - 1)) & (mm <= ratio * bb + ratio - 1), 1.0, 0.0).astype(BF16)
    p_hi = psum.astype(BF16)
    p_lo = (psum - p_hi.astype(F32)).astype(BF16)
    imp = _dot(a_t, p_hi) + _dot(a_t, p_lo)

    blk = lax.broadcasted_iota(jnp.int32, (LANES, tq), 0)
    t_s = q0 + lax.broadcasted_iota(jnp.int32, (LANES, tq), 1)
    cur = jnp.right_shift(t_s, SLC_SHIFT)
    forced = (blk == 0) | (blk == cur) | (blk == cur - 1)
    valid = blk * SLC_BLOCK <= t_s
    cand = jnp.where(forced, jnp.inf, jnp.where(valid, imp, -jnp.inf))
    blkf = blk.astype(F32)

    def pick_one(_, carry):
        cand, sel = carry
        m = jnp.max(cand, axis=0, keepdims=True)
        idx = jnp.min(jnp.where(cand == m, blkf, float(LANES)), axis=0, keepdims=True)
        pick = blkf == idx
        return jnp.where(pick, -jnp.inf, cand), jnp.where(pick, 1.0, sel)

    _, sel = lax.fori_loop(0, n_top, pick_one, (cand, jnp.zeros((LANES, tq), F32)))
    pen_t = jnp.where(valid, jnp.where(sel > 0.0, 0.0, NEG), NEG)
    pen = pen_t.T.astype(BF16)
    for h in range(NSA_HPG):
        qaug_o[0, 0, 0, tq * h:tq * (h + 1), 0:LANES] = pen


def _nsa_select(qa, kc, vct, *, n_top):
    B, S, _ = qa.shape
    G = NSA_GROUPS
    tq = Q_BLOCK
    nq = S // tq
    ncp = kc.shape[2]
    gw = NSA_HPG * NSA_D
    return pl.pallas_call(
        functools.partial(_nsa_select_kernel, n_top=n_top),
        grid=(B, G, nq),
        in_specs=[pl.BlockSpec((1, tq, gw), lambda b, g, i: (b, i, g)),
                  pl.BlockSpec((1, 1, ncp, 2 * LANES), lambda b, g, i: (b, g, 0, 0)),
                  pl.BlockSpec((1, 1, NSA_D, ncp), lambda b, g, i: (b, g, 0, 0))],
        out_specs=(pl.BlockSpec((1, tq, gw), lambda b, g, i: (b, i, g)),
                   pl.BlockSpec((1, 1, 1, NSA_HPG * tq, 2 * LANES), lambda b, g, i: (b, g, i, 0, 0))),
        out_shape=(jax.ShapeDtypeStruct((B, S, G * gw), F32),
                   jax.ShapeDtypeStruct((B, G, nq, NSA_HPG * tq, 2 * LANES), BF16)),
        scratch_shapes=[pltpu.VMEM((NSA_HPG * tq, 2 * LANES), BF16),
                        pltpu.VMEM((ncp, NSA_HPG * tq), F32), pltpu.VMEM((ncp, NSA_HPG * tq), BF16)],
        compiler_params=_params("parallel", "parallel", "parallel"),
        name="nsa_select",
    )(qa, kc, vct)


CHUNK = 64


def _scores(q, k, buf, row0=0):
    buf[0][row0:, :k.shape[0]] = _dot_nt(q[row0:], k)


def _softmax_pv(buf, v, m_scr, acc_scr, bias_fn=None, bias_key=None, row0=0):
    s_scr, p_scr = buf
    rows, n = m_scr.shape[0], v.shape[0]
    biases = {}
    for r0 in range(row0, rows, CHUNK):
        rs = slice(r0, r0 + CHUNK)
        s = s_scr[rs, :n]
        if bias_fn is not None:
            key = r0 if bias_key is None else bias_key(r0)
            if key not in biases:
                biases[key] = bias_fn(r0)
            if biases[key] is not None:
                s = s + biases[key]
        m_old = m_scr[rs]
        m_new = jnp.maximum(m_old, jnp.broadcast_to(jnp.max(s, axis=-1, keepdims=True), m_old.shape))
        m_scr[rs] = m_new
        alpha = jnp.exp2(m_old - m_new)
        for c in range(0, acc_scr.shape[1], LANES):
            acc_scr[rs, c:c + LANES] = acc_scr[rs, c:c + LANES] * alpha
        for c in range(0, n, LANES):
            p_scr[rs, c:c + LANES] = jnp.exp2((s[:, c:c + LANES] - m_new).astype(BF16))
    acc_scr[row0:] += _dot(p_scr[row0:, :n], v)


def _causal_flash(q, k_ref, v_ref, n_full, tk, buf_a, buf_b, m_scr, acc_scr, tail, bias_key=None,
                  overlap_fn=None, n_full_even=False):
    k_tile = lambda j: k_ref[pl.ds(pl.multiple_of(j * tk, tk), tk), :]
    v_tile = lambda j: v_ref[pl.ds(pl.multiple_of(j * tk, tk), tk), :]
    n_pairs = lax.div(n_full, 2)

    def finish(j, steps):
        bufs = (buf_a, buf_b)
        for k, (row0, bias_fn) in enumerate(steps):
            if k + 1 < len(steps):
                _scores(q, k_tile(j + k + 1), bufs[(k + 1) % 2], row0=steps[k + 1][0])
            _softmax_pv(bufs[k % 2], v_tile(j + k), m_scr, acc_scr, bias_fn=bias_fn, bias_key=bias_key, row0=row0)

    _scores(q, k_tile(0), buf_a)
    if overlap_fn is not None:
        overlap_fn()

    def pair(jj, carry):
        j = 2 * jj
        _scores(q, k_tile(j + 1), buf_b)
        _softmax_pv(buf_a, v_tile(j), m_scr, acc_scr)
        _scores(q, k_tile(j + 2), buf_a)
        _softmax_pv(buf_b, v_tile(j + 1), m_scr, acc_scr)
        return carry

    lax.fori_loop(0, n_pairs, pair, 0)
    j = 2 * n_pairs
    if n_full_even:
        finish(j, tail)
    else:
        pl.when(j < n_full)(lambda: finish(j, [(0, None)] + tail))
        pl.when(j == n_full)(lambda: finish(j, tail))


def _causal_bias(t0, pos0, n):
    t = t0 + lax.broadcasted_iota(jnp.int32, (CHUNK, n), 0)
    pos = pos0 + lax.broadcasted_iota(jnp.int32, (CHUNK, n), 1)
    return jnp.where(pos <= t, 0.0, NEG)


def _nsa_attend_kernel(qaug_ref, ks_ref, vs_ref, kw_ref, vw_ref, oc_ref, gs_ref, o_ref,
                       s_a, s_b, p_a, p_b, m_scr, mw_scr, accs_scr, accw_scr, *, tk):
    i = pl.program_id(2)
    buf_a, buf_b = (s_a, p_a), (s_b, p_b)
    qb = qaug_ref.shape[2]
    blk_rows = NSA_HPG * Q_BLOCK
    rows = qb * blk_rows
    q = qaug_ref[0, 0].reshape(rows, 2 * LANES)
    q0 = i * (qb * Q_BLOCK)
    row_dt = lambda r0: (r0 // blk_rows) * Q_BLOCK + r0 % Q_BLOCK

    offw = jnp.maximum(q0 - WINDOW, 0)
    w_off = lambda w: pl.multiple_of(offw + w * tk, Q_BLOCK)

    def window_bias(w):
        def bias(r0):
            d = (q0 + row_dt(r0) + lax.broadcasted_iota(jnp.int32, (CHUNK, tk), 0)
                 - (w_off(w) + lax.broadcasted_iota(jnp.int32, (CHUNK, tk), 1)))
            return jnp.where(d >= 0, jnp.where(d < WINDOW, 0.0, NEG), NEG)
        return bias

    def window_softmax(w, buf):
        _softmax_pv(buf, vw_ref[0, 0, pl.ds(w_off(w), tk), :], mw_scr, accw_scr,
                    bias_fn=window_bias(w), bias_key=row_dt)

    mw_scr[...] = jnp.full((rows, LANES), NEG, F32)
    accw_scr[...] = jnp.zeros(accw_scr.shape, F32)
    _scores(q, kw_ref[0, 0, pl.ds(w_off(0), tk), :], buf_a)
    _scores(q, kw_ref[0, 0, pl.ds(w_off(1), tk), :], buf_b)
    window_softmax(0, buf_a)

    m_scr[...] = jnp.full((rows, LANES), NEG, F32)
    accs_scr[...] = jnp.zeros(accs_scr.shape, F32)
    n_full = lax.div(q0, tk)
    _causal_flash(q, ks_ref.at[0, 0], vs_ref.at[0, 0], n_full, tk, buf_a, buf_b, m_scr, accs_scr,
                  tail=[(0, lambda r0: _causal_bias(q0 + row_dt(r0), n_full * tk, tk))], bias_key=row_dt,
                  overlap_fn=lambda: window_softmax(1, buf_b))

    gw = NSA_HPG * NSA_D
    gs = gs_ref[0, 0]
    src = lax.broadcasted_iota(jnp.int32, (LANES, 3 * gw), 0)
    dst = lax.broadcasted_iota(jnp.int32, (LANES, 3 * gw), 1)
    branch = jnp.right_shift(dst, gw.bit_length() - 1)
    head = jnp.right_shift(dst & (gw - 1), NSA_D.bit_length() - 1)
    spread = jnp.where(src == 3 * head + branch, 1.0, 0.0).astype(BF16)
    gs_hi = gs.astype(BF16)
    gs_lo = (gs - gs_hi.astype(F32)).astype(BF16)
    g_exp = _dot(gs_hi, spread) + _dot(gs_lo, spread)

    lane = lax.broadcasted_iota(jnp.int32, (Q_BLOCK, LANES), 1)

    def normalised(acc_ref, c, h2):
        r0 = blk_rows * c + Q_BLOCK * 2 * h2
        even = acc_ref[r0:r0 + Q_BLOCK, :]
        odd = acc_ref[r0 + Q_BLOCK:r0 + 2 * Q_BLOCK, :]
        return jnp.where(lane < NSA_D, even[:, :LANES] / even[:, LANES:], odd[:, LANES:] / odd[:, :LANES])

    for c in range(qb):
        qs = slice(Q_BLOCK * c, Q_BLOCK * (c + 1))
        for h2 in range(NSA_HPG // 2):
            ls = slice(LANES * h2, LANES * (h2 + 1))
            gate = lambda j: g_exp[qs, gw * j + LANES * h2:gw * j + LANES * (h2 + 1)]
            o = (gate(0) * oc_ref[0, qs, ls] + gate(1) * normalised(accs_scr, c, h2)
                 + gate(2) * normalised(accw_scr, c, h2))
            o_ref[0, qs, ls] = o.astype(BF16)


def _nsa_attend(qaug, ks, vs, kw, vw, oc, gs, *, tk=512, qb=4):
    B, G, nq = qaug.shape[:3]
    S = ks.shape[2]
    tq = qb * Q_BLOCK
    tk = min(tk, S)
    gw = NSA_HPG * NSA_D
    rows = NSA_HPG * tq
    nw = WINDOW + tq
    assert S >= nw and nq % qb == 0 and tk % tq == 0 and nw == 2 * tk
    kv = lambda w: pl.BlockSpec((1, 1, S, w), lambda b, g, i: (b, g, 0, 0), pipeline_mode=pl.Buffered(1))
    return pl.pallas_call(
        functools.partial(_nsa_attend_kernel, tk=tk),
        grid=(B, G, nq // qb),
        in_specs=[pl.BlockSpec((1, 1, qb, NSA_HPG * Q_BLOCK, 2 * LANES), lambda b, g, i: (b, g, i, 0, 0)),
                  kv(2 * LANES), kv(2 * LANES), kv(2 * LANES), kv(2 * LANES),
                  pl.BlockSpec((1, tq, gw), lambda b, g, i: (b, i, g)),
                  pl.BlockSpec((1, 1, tq, LANES), lambda b, g, i: (b, g, i, 0))],
        out_specs=pl.BlockSpec((1, tq, gw), lambda b, g, i: (b, i, g)),
        out_shape=jax.ShapeDtypeStruct((B, S, G * gw), BF16),
        scratch_shapes=[pltpu.VMEM((rows, tk), F32), pltpu.VMEM((rows, tk), F32),
                        pltpu.VMEM((rows, tk), BF16), pltpu.VMEM((rows, tk), BF16),
                        pltpu.VMEM((rows, LANES), F32), pltpu.VMEM((rows, LANES), F32),
                        pltpu.VMEM((rows, 2 * LANES), F32), pltpu.VMEM((rows, 2 * LANES), F32)],
        compiler_params=_params("parallel", "parallel", "arbitrary"),
        name="nsa_attend",
    )(qaug, ks, vs, kw, vw, oc, gs)


def _mla_kernel(q_ref, k_ref, v_ref, o_ref, s_a, s_b, p_a, p_b, m_scr, acc_scr, *, tk):
    i = pl.program_id(2)
    q = q_ref[0]
    m_scr[...] = jnp.full(m_scr.shape, NEG, F32)
    acc_scr[...] = jnp.zeros(acc_scr.shape, F32)
    tail = [(0, lambda r0: _causal_bias(r0, 0, tk) if r0 < tk else None),
            (tk, lambda r0: _causal_bias(r0 - tk, 0, tk))]
    _causal_flash(q, k_ref.at[0], v_ref.at[0], 2 * i, tk, (s_a, p_a), (s_b, p_b), m_scr, acc_scr,
                  tail=tail, n_full_even=True)
    acc = acc_scr[...]
    o_ref[0] = (acc / acc[:, MLA_V:MLA_V + 1]).astype(BF16)


def _mla_attend(qcat, kcat, vcat, *, tk=512):
    B, S, _ = qcat.shape
    tk = min(tk, S // 2)
    t = 2 * tk
    H = MLA_HEADS
    kv = pl.BlockSpec((1, S, LANES), lambda b, h, i: (b, 0, h))
    return pl.pallas_call(
        functools.partial(_mla_kernel, tk=tk),
        grid=(B, H, S // t),
        in_specs=[pl.BlockSpec((1, t, LANES), lambda b, h, i: (b, i, h)), kv, kv],
        out_specs=pl.BlockSpec((1, t, LANES), lambda b, h, i: (b, i, h)),
        out_shape=jax.ShapeDtypeStruct((B, S, H * LANES), BF16),
        scratch_shapes=[pltpu.VMEM((t, tk), F32), pltpu.VMEM((t, tk), F32),
                        pltpu.VMEM((t, tk), BF16), pltpu.VMEM((t, tk), BF16),
                        pltpu.VMEM((t, LANES), F32), pltpu.VMEM((t, LANES), F32)],
        compiler_params=_params("parallel", "parallel", "arbitrary"),
        name="mla_attend",
    )(qcat, kcat, vcat)


def _merge_kernel(x_ref, g_ref, gm_ref, oa_ref, ob_ref, wa_ref, wb_ref, wo_ref, o_ref):
    D = x_ref.shape[2]
    gm = gm_ref[0]
    y = gm[:, :D] * _dot(oa_ref[0], wa_ref[...]) + gm[:, D:] * _dot(ob_ref[0], wb_ref[...])
    o_ref[0] = x_ref[0] + g_ref[0] * _dot(y.astype(BF16), wo_ref[...])


def _merge(x, g, gm, oa, ob, wa, wb, wo, *, tm=512):
    B, S, D = x.shape
    row = lambda w: pl.BlockSpec((1, tm, w), lambda b, i: (b, i, 0))
    res = lambda a: _resident(a.shape, lambda b, i: (0, 0))
    return pl.pallas_call(
        _merge_kernel,
        grid=(B, S // tm),
        in_specs=[row(D), pl.BlockSpec((1, 1, D), lambda b, i: (b, 0, 0)), row(2 * D),
                  row(oa.shape[2]), row(ob.shape[2]), res(wa), res(wb), res(wo)],
        out_specs=row(D),
        out_shape=jax.ShapeDtypeStruct((B, S, D), F32),
        compiler_params=_params("parallel", "parallel"),
        name="merge",
    )(x, g, gm, oa, ob, wa, wb, wo)


def _pad_heads(w, n_heads, width):
    k = w.shape[0]
    w = w.reshape(k, n_heads, width)
    return jnp.pad(w, ((0, 0), (0, 0), (0, LANES - width))).reshape(k, n_heads * LANES)


def _rot_cols(w):
    half = w.shape[-1] // 2
    return jnp.concatenate([-w[..., half:], w[..., :half]], axis=-1)


def _prep_inproj_weights(w_in, w_uq, w_uk, w_uv):
    D = w_in.shape[0]
    o = 0
    seg = {}
    for name, width in (("qa", 512), ("kc", 128), ("vc", 128), ("ks", 128), ("vs", 128), ("kw", 128),
                        ("vw", 128), ("gnsa", 3 * NSA_HEADS), ("cq", MLA_Q_RANK), ("ckv", MLA_KV_RANK),
                        ("kr", MLA_ROPE), ("gm", 2048)):
        seg[name] = w_in[:, o:o + width]
        o += width
    per_g = 3 * NSA_HPG
    gn = jnp.concatenate(
        [jnp.pad(seg["gnsa"][:, per_g * g:per_g * (g + 1)], ((0, 0), (0, LANES - per_g)))
         for g in range(NSA_GROUPS)], axis=1)
    z64 = jnp.zeros((D, MLA_NOPE), F32)
    z32 = jnp.zeros((D, LANES - MLA_NOPE - MLA_ROPE), F32)
    kr = jnp.concatenate([z64, seg["kr"], z32, z64, _rot_cols(seg["kr"]), z32], axis=1)
    w_all = jnp.concatenate([seg["qa"], seg["kc"], seg["vc"], seg["ks"], seg["vs"], seg["kw"], seg["vw"],
                             gn, seg["cq"], seg["ckv"], kr, seg["gm"]], axis=1).astype(BF16)
    assert w_all.shape[1] == _SEG_TOTAL
    hd = MLA_NOPE + MLA_ROPE
    uq = w_uq.reshape(-1, MLA_HEADS, hd)
    uq_rot = jnp.concatenate([jnp.zeros_like(uq[..., :MLA_NOPE]), _rot_cols(uq[..., MLA_NOPE:])], axis=-1)
    wqa = _pad_heads(w_uq, MLA_HEADS, hd).astype(BF16)
    wqb = _pad_heads(uq_rot.reshape(-1, MLA_HEADS * hd), MLA_HEADS, hd).astype(BF16)
    wuk = _pad_heads(w_uk, MLA_HEADS, MLA_NOPE).astype(BF16)
    wuv = _pad_heads(w_uv, MLA_HEADS, MLA_V).astype(BF16)
    return w_all, wqa, wqb, wuk, wuv


def _prep_compress_weights(k_w1, k_w2, k_pe, v_w1, v_w2, v_pe):
    d = NSA_D
    half = CMP_STRIDE

    def blockdiag(mats):
        n = len(mats)
        z = jnp.zeros_like(mats[0])
        rows = [jnp.concatenate([mats[r] if c == r else z for c in range(n)], axis=-1) for r in range(n)]
        return jnp.concatenate(rows, axis=-2)

    def first(w1, lo):
        return w1.reshape(CMP_BLOCK, d, d)[lo:lo + half]

    def w1_half(lo):
        k, v = first(k_w1, lo), first(v_w1, lo)
        return blockdiag([k, k, v, v]).reshape(half * 4 * d, 4 * d).astype(BF16)

    def pe_half(lo):
        k, v = k_pe[lo:lo + half], v_pe[lo:lo + half]
        return jnp.concatenate([k, k, v, v], axis=-1).reshape(1, half * 4 * d)

    w2 = blockdiag([k_w2, k_w2, v_w2, v_w2]).astype(BF16)
    return pe_half(0), pe_half(half), w1_half(0), w1_half(half), w2


def _rope_tables(S):
    half = MLA_ROPE // 2
    inv_freq = ROPE_THETA ** (-jnp.arange(half, dtype=F32) / half)
    ang = jnp.arange(S, dtype=F32)[:, None] * inv_freq[None, :]
    cos, sin = jnp.cos(ang), jnp.sin(ang)
    ones = jnp.ones((S, MLA_NOPE), F32)
    z_n = jnp.zeros((S, MLA_NOPE), F32)
    z_p = jnp.zeros((S, LANES - MLA_NOPE - MLA_ROPE), F32)
    ctab = jnp.concatenate([ones, cos, cos, z_p], axis=1)
    stab = jnp.concatenate([z_n, sin, sin, z_p], axis=1)
    return ctab, stab


def kernel(x, c, w_ada, b_ada, ffn1_gate, ffn1_up, ffn1_down, ffn2_gate, ffn2_up, ffn2_down, w_in, cmpk_w1, cmpk_w2, cmpk_pe, cmpv_w1, cmpv_w2, cmpv_pe, mla_q_norm, mla_w_uq, mla_kv_norm, mla_w_uk, mla_w_uv, w_branch_a, w_branch_b, w_out, final_norm):
    B, S, D = x.shape
    L = w_ada.shape[0]
    assert NSA_GROUPS == 2 and S // SLC_BLOCK <= LANES and S // (2 * CMP_STRIDE) <= 256 and S % 512 == 0
    n_top = min(N_SELECT, S // SLC_BLOCK)
    ctab, stab = _rope_tables(S)
    mod = _adaln(c, w_ada, b_ada).reshape(L, B, N_MOD, 1, D)
    bf = lambda w: w.astype(BF16)
    for l in range(L):
        sh1, sc1, g1, sh2, sc2, g2, sh3, sc3, g3 = (mod[l, :, k] for k in range(N_MOD))
        x = _ffn(x, sh1, sc1, g1, bf(ffn1_gate[l]), bf(ffn1_up[l]), bf(ffn1_down[l]), final_norm, final=False)

        w_all, wqa, wqb, wuk, wuv = _prep_inproj_weights(w_in[l], mla_w_uq[l], mla_w_uk[l], mla_w_uv[l])
        (qa, kcvc, ks, vs, kw, vw, gs, qcat, kcat, vcat, gm) = _inproj(
            x, sh2, sc2, w_all, wqa, wqb, wuk, wuv,
            mla_q_norm[l].reshape(1, -1), mla_kv_norm[l].reshape(1, -1), ctab, stab)
        kc, vct = _compress(kcvc, *_prep_compress_weights(
            cmpk_w1[l], cmpk_w2[l], cmpk_pe[l], cmpv_w1[l], cmpv_w2[l], cmpv_pe[l]))
        oc, qaug = _nsa_select(qa, kc, vct, n_top=n_top)
        oa = _nsa_attend(qaug, ks, vs, kw, vw, oc, gs)
        ob = _mla_attend(qcat, kcat, vcat)
        wb_pad = jnp.pad(w_branch_b[l].reshape(MLA_HEADS, MLA_V, D),
                         ((0, 0), (0, LANES - MLA_V), (0, 0))).reshape(MLA_HEADS * LANES, D)
        x = _merge(x, g2, gm, oa, ob, bf(w_branch_a[l]), bf(wb_pad), bf(w_out[l]))

        x = _ffn(x, sh3, sc3, g3, bf(ffn2_gate[l]), bf(ffn2_up[l]), bf(ffn2_down[l]), final_norm,
                 final=(l == L - 1))
    return x
```

```python
import functools
import math

import ml_dtypes
import numpy as np
import jax
import jax.numpy as jnp
from jax import lax
from jax.experimental import pallas as pl
from jax.experimental.pallas import tpu as pltpu

F32 = jnp.float32
BF16 = jnp.bfloat16

NSA_HEADS = 8
NSA_GROUPS = 2
NSA_HPG = NSA_HEADS // NSA_GROUPS
NSA_D = 64
CMP_BLOCK = 32
CMP_STRIDE = 16
SLC_BLOCK = 64
N_SELECT = 16
WINDOW = 512
MLA_HEADS = 8
MLA_NOPE = 64
MLA_ROPE = 32
MLA_V = 64
MLA_Q_RANK = 384
MLA_KV_RANK = 256
ROPE_THETA = 10000.0
N_MOD = 9
Q_BLOCK = 128
NORM_EPS = 1e-6

LANES = 128
MXU_WIDTH = 256
SLC_SHIFT = SLC_BLOCK.bit_length() - 1
assert 1 << SLC_SHIFT == SLC_BLOCK
LOG2E = math.log2(math.e)
SLOPES = [2.0 ** (-8.0 * (k + 1) / NSA_HEADS) for k in range(NSA_HEADS)]
N_PIECES = 3


def _bf16_pieces(v):
    out = []
    for _ in range(N_PIECES):
        p = float(np.float32(v).astype(ml_dtypes.bfloat16))
        out.append(p)
        v -= p
    return out


SLOPE_PIECES = [_bf16_pieces(v * LOG2E) for v in SLOPES]
NEG = -1e30
VMEM_LIMIT = 56 * 1024 * 1024


def _rms(xf):
    return xf * lax.rsqrt(jnp.mean(xf * xf, -1, keepdims=True) + NORM_EPS)


def _sigmoid(x):
    return 1.0 / (1.0 + jnp.exp(-x))


def _dot(a, b):
    return jnp.dot(a, b, preferred_element_type=F32)


def _dot_nt(a, b, precision=None):
    return lax.dot_general(a, b, (((1,), (1,)), ((), ())), precision=precision,
                           preferred_element_type=F32)


def _params(*sem):
    return pltpu.CompilerParams(dimension_semantics=sem, vmem_limit_bytes=VMEM_LIMIT)


def _resident(shape, index_map):
    return pl.BlockSpec(shape, index_map, pipeline_mode=pl.Buffered(1))


def _adaln_kernel(c_ref, w_ref, b_ref, o_ref):
    c = c_ref[...]
    a = c * _sigmoid(c)
    o_ref[0] = jnp.dot(a, w_ref[0], precision=lax.Precision.HIGHEST,
                       preferred_element_type=F32) + b_ref[0]


def _adaln(c, w_ada, b_ada):
    L, D, N = w_ada.shape
    B = c.shape[0]
    rows = 8
    c_pad = jnp.zeros((rows, D), F32).at[:B].set(c)
    tn = N // 8
    out = pl.pallas_call(
        _adaln_kernel,
        grid=(L, N // tn),
        in_specs=[pl.BlockSpec((rows, D), lambda l, j: (0, 0)),
                  pl.BlockSpec((1, D, tn), lambda l, j: (l, 0, j)),
                  pl.BlockSpec((1, 1, tn), lambda l, j: (l, 0, j))],
        out_specs=pl.BlockSpec((1, rows, tn), lambda l, j: (l, 0, j)),
        out_shape=jax.ShapeDtypeStruct((L, rows, N), F32),
        compiler_params=_params("parallel", "parallel"),
        name="adaln_mod",
    )(c_pad, w_ada, b_ada.reshape(L, 1, N))
    return out[:, :B]


def _ffn_kernel(x_ref, sh_ref, sc_ref, g_ref, wg_ref, wu_ref, wd_ref, fn_ref, o_ref, *, splits, final):
    x = x_ref[0]
    u = (_rms(x) * (1.0 + sc_ref[0]) + sh_ref[0]).astype(BF16)
    acc = jnp.zeros(x.shape, F32)
    for f0, f1 in zip(splits[:-1], splits[1:]):
        hg = _dot(u, wg_ref[:, f0:f1])
        hu = _dot(u, wu_ref[:, f0:f1])
        a = (hg * _sigmoid(hg) * hu).astype(BF16)
        acc = acc + _dot(a, wd_ref[f0:f1, :])
    y = x + (0.5 * g_ref[0]) * acc
    if final:
        y = _rms(y) * fn_ref[...]
    o_ref[0] = y


def _ffn(x, sh, sc, g, wg, wu, wd, final_norm, *, final, tm=512):
    B, S, D = x.shape
    n_ff = wg.shape[1]
    cut = (n_ff // 2 + MXU_WIDTH - 1) // MXU_WIDTH * MXU_WIDTH
    splits = (0, cut, n_ff) if 0 < cut < n_ff else (0, n_ff)
    vec = pl.BlockSpec((1, 1, D), lambda b, i: (b, 0, 0))
    return pl.pallas_call(
        functools.partial(_ffn_kernel, splits=splits, final=final),
        grid=(B, S // tm),
        in_specs=[pl.BlockSpec((1, tm, D), lambda b, i: (b, i, 0)), vec, vec, vec,
                  _resident((D, n_ff), lambda b, i: (0, 0)),
                  _resident((D, n_ff), lambda b, i: (0, 0)),
                  _resident((n_ff, D), lambda b, i: (0, 0)),
                  pl.BlockSpec((1, D), lambda b, i: (0, 0))],
        out_specs=pl.BlockSpec((1, tm, D), lambda b, i: (b, i, 0)),
        out_shape=jax.ShapeDtypeStruct((B, S, D), F32),
        compiler_params=_params("parallel", "parallel"),
        name="ffn_final" if final else "ffn",
    )(x, sh, sc, g, wg, wu, wd, final_norm.reshape(1, D))


_SEG = {}
_o = 0
for _name, _w in (("qa", 512), ("kcvc", 256), ("kv4", 512), ("gnsa", 256), ("cq", MLA_Q_RANK),
                  ("ckv", MLA_KV_RANK), ("kr", 256), ("gm", 2048)):
    _SEG[_name] = (_o, _o + _w)
    _o += _w
_SEG_TOTAL = _o


def _inproj_kernel(x_ref, sh_ref, sc_ref, w_ref, wqa_ref, wqb_ref, wuk_ref, wuv_ref, qn_ref, kvn_ref,
                   ctab_ref, stab_ref,
                   qa_o, kcvc_o, ks_o, vs_o, kw_o, vw_o, gs_o, qcat_o, kcat_o, vcat_o, gm_o, *, tm):
    i = pl.program_id(1)
    x = x_ref[0]
    u = (_rms(x) * (1.0 + sc_ref[0]) + sh_ref[0]).astype(BF16)

    def proj(name):
        a, b = _SEG[name]
        return _dot(u, w_ref[:, a:b])

    qa_o[0] = proj("qa").astype(BF16)
    kcvc_o[0] = proj("kcvc")

    kv = proj("kv4")
    lane = lax.broadcasted_iota(jnp.int32, (tm, LANES), 1)
    pos = i * tm + lax.broadcasted_iota(jnp.int32, (tm, LANES), 0)
    blk = jnp.right_shift(pos, SLC_SHIFT)
    rem = pos - blk * SLC_BLOCK
    onehot = jnp.where(lane == blk, 1.0, 0.0).astype(BF16)
    zeros = jnp.zeros((tm, LANES), BF16)
    kextra = jnp.where(lane < NSA_D + N_PIECES, rem.astype(F32),
                       jnp.where(lane < NSA_D + 2 * N_PIECES, blk.astype(F32), 0.0))
    for g in range(NSA_GROUPS):
        for src, k_o, v_o, first in ((0, ks_o, vs_o, onehot), (2, kw_o, vw_o, zeros)):
            kk = kv[:, LANES * src:LANES * (src + 1)]
            vv = kv[:, LANES * (src + 1):LANES * (src + 2)]
            vr = pltpu.roll(vv, NSA_D, 1)
            v_lo, v_hi = (vv, vr) if g == 0 else (vr, vv)
            if g == 1:
                kk = pltpu.roll(kk, NSA_D, 1)
            k_o[0, g, :, 0:LANES] = first
            k_o[0, g, :, LANES:2 * LANES] = jnp.where(lane < NSA_D, kk, kextra).astype(BF16)
            v_o[0, g, :, 0:LANES] = jnp.where(lane < NSA_D, v_lo, 1.0).astype(BF16)
            v_o[0, g, :, LANES:2 * LANES] = jnp.where(lane < NSA_D, 1.0, v_hi).astype(BF16)

    gs = _sigmoid(proj("gnsa"))
    for g in range(NSA_GROUPS):
        gs_o[0, g] = gs[:, LANES * g:LANES * (g + 1)]

    ctab = ctab_ref[...]
    stab = stab_ref[...]
    scale = (MLA_NOPE + MLA_ROPE) ** -0.5 * LOG2E
    cqn =(_rms(proj("cq")) * qn_ref[...]).astype(BF16)
    q_a = _dot(cqn, wqa_ref[...])
    q_b = _dot(cqn, wqb_ref[...])
    ckvn = (_rms(proj("ckv")) * kvn_ref[...]).astype(BF16)
    kn = _dot(ckvn, wuk_ref[...])
    vv = _dot(ckvn, wuv_ref[...])
    kr = proj("kr")
    krope = kr[:, :LANES] * ctab + kr[:, LANES:] * stab
    for h in range(MLA_HEADS):
        sl = slice(LANES * h, LANES * (h + 1))
        qcat_o[0, :, sl] = ((q_a[:, sl] * ctab + q_b[:, sl] * stab) * scale).astype(BF16)
        kcat_o[0, :, sl] = (kn[:, sl] + krope).astype(BF16)
        vcat_o[0, :, sl] = jnp.where(lane == MLA_V, 1.0, vv[:, sl]).astype(BF16)

    gm_o[0] = _sigmoid(proj("gm"))


def _inproj(x, sh, sc, w_all, wqa, wqb, wuk, wuv, qn, kvn, ctab, stab, *, tm=512):
    B, S, D = x.shape
    G = NSA_GROUPS
    grid = (B, S // tm)
    vec = pl.BlockSpec((1, 1, D), lambda b, i: (b, 0, 0))

    def res(a):
        return _resident(a.shape, lambda b, i: (0,) * a.ndim)

    row = lambda w: pl.BlockSpec((1, tm, w), lambda b, i: (b, i, 0))
    grp = lambda w: pl.BlockSpec((1, G, tm, w), lambda b, i: (b, 0, i, 0))
    tab = pl.BlockSpec((tm, LANES), lambda b, i: (i, 0))
    out_shape = (
        jax.ShapeDtypeStruct((B, S, 512), BF16),
        jax.ShapeDtypeStruct((B, S, 256), F32),
        jax.ShapeDtypeStruct((B, G, S, 256), BF16),
        jax.ShapeDtypeStruct((B, G, S, 256), BF16),
        jax.ShapeDtypeStruct((B, G, S, 256), BF16),
        jax.ShapeDtypeStruct((B, G, S, 256), BF16),
        jax.ShapeDtypeStruct((B, G, S, 128), F32),
        jax.ShapeDtypeStruct((B, S, MLA_HEADS * LANES), BF16),
        jax.ShapeDtypeStruct((B, S, MLA_HEADS * LANES), BF16),
        jax.ShapeDtypeStruct((B, S, MLA_HEADS * LANES), BF16),
        jax.ShapeDtypeStruct((B, S, 2048), F32),
    )
    out_specs = (row(512), row(256), grp(256), grp(256), grp(256), grp(256), grp(128),
                 row(MLA_HEADS * LANES), row(MLA_HEADS * LANES), row(MLA_HEADS * LANES), row(2048))
    return pl.pallas_call(
        functools.partial(_inproj_kernel, tm=tm),
        grid=grid,
        in_specs=[pl.BlockSpec((1, tm, D), lambda b, i: (b, i, 0)), vec, vec,
                  res(w_all), res(wqa), res(wqb), res(wuk), res(wuv), res(qn), res(kvn), tab, tab],
        out_specs=out_specs,
        out_shape=out_shape,
        compiler_params=_params("parallel", "parallel"),
        name="in_proj",
    )(x, sh, sc, w_all, wqa, wqb, wuk, wuv, qn, kvn, ctab, stab)


def _compress_kernel(a_ref, pea_ref, peb_ref, wa_ref, wb_ref, w2_ref, kcat_o, vct_o):
    a = a_ref[0]
    n = a.shape[0]
    pa = _dot((a + pea_ref[...]).astype(BF16), wa_ref[...])
    pb = _dot((a + peb_ref[...]).astype(BF16), wb_ref[...])
    h = pa + pltpu.roll(pb, n - 1, 0)
    h = 0.5 * h * (1.0 + jnp.tanh(0.7978845608028654 * (h + 0.044715 * (h * h * h))))
    out = _dot(h.astype(BF16), w2_ref[...])
    out_t = out.T
    lane = lax.broadcasted_iota(jnp.int32, (n, LANES), 1)
    row = lax.broadcasted_iota(jnp.int32, (n, LANES), 0)
    pair = jnp.right_shift(row, 1)
    within = (row - 2 * pair) * CMP_STRIDE + (CMP_BLOCK - 1)
    kextra = jnp.where(lane < NSA_D + N_PIECES, pair.astype(F32),
                       jnp.where(lane < NSA_D + 2 * N_PIECES, within.astype(F32), 0.0))
    kk = out[:, 0:LANES] * (NSA_D ** -0.5 * LOG2E)
    for g in range(NSA_GROUPS):
        x = kk if g == 0 else pltpu.roll(kk, NSA_D, 1)
        lo = x - x.astype(BF16).astype(F32)
        kcat_o[0, g, :, 0:LANES] = jnp.where(lane < NSA_D, x, kextra).astype(BF16)
        kcat_o[0, g, :, LANES:2 * LANES] = jnp.where(lane < NSA_D, lo, 0.0).astype(BF16)
        vct_o[0, g] = out_t[2 * NSA_D + NSA_D * g:2 * NSA_D + NSA_D * (g + 1), :].astype(BF16)


def _compress(kcvc, pea, peb, wa, wb, w2):
    B, S, W = kcvc.shape
    n = S // CMP_STRIDE
    a = kcvc.reshape(B, n, CMP_STRIDE * W)
    G = NSA_GROUPS
    full = lambda arr: _resident(arr.shape, lambda b: (0,) * arr.ndim)
    return pl.pallas_call(
        _compress_kernel,
        grid=(B,),
        in_specs=[pl.BlockSpec((1, n, CMP_STRIDE * W), lambda b: (b, 0, 0)),
                  full(pea), full(peb), full(wa), full(wb), full(w2)],
        out_specs=(pl.BlockSpec((1, G, n, 2 * LANES), lambda b: (b, 0, 0, 0)),
                   pl.BlockSpec((1, G, NSA_D, n), lambda b: (b, 0, 0, 0))),
        out_shape=(jax.ShapeDtypeStruct((B, G, n, 2 * LANES), BF16),
                   jax.ShapeDtypeStruct((B, G, NSA_D, n), BF16)),
        compiler_params=_params("parallel"),
        name="compress",
    )(a, pea, peb, wa, wb, w2)


def _nsa_select_kernel(q_ref, kcat_ref, vct_ref, oc_o, qaug_o, used_o, qc_scr, s_scr, p_scr, *, n_top):
    g = pl.program_id(1)
    i = pl.program_id(2)
    tq = q_ref.shape[1]
    ncp = kcat_ref.shape[2]
    q0 = i * tq
    scale = NSA_D ** -0.5

    lane = lax.broadcasted_iota(jnp.int32, (tq, LANES), 1)
    qf = q_ref[0].astype(F32)
    for h in range(NSA_HPG):
        rs = slice(tq * h, tq * (h + 1))
        x = qf[:, LANES * (h // 2):LANES * (h // 2 + 1)]
        if h % 2 == 1:
            x = pltpu.roll(x, NSA_D, 1)
        extra = jnp.zeros((tq, LANES), F32)
        extra_c = jnp.zeros((tq, LANES), F32)
        for k in range(N_PIECES):
            piece = jnp.where(g == 0, SLOPE_PIECES[h][k], SLOPE_PIECES[NSA_HPG + h][k])
            extra = jnp.where(lane == NSA_D + k, piece,
                              jnp.where(lane == NSA_D + N_PIECES + k, piece * SLC_BLOCK, extra))
            extra_c = jnp.where(lane == NSA_D + k, piece * (2 * CMP_STRIDE),
                                jnp.where(lane == NSA_D + N_PIECES + k, piece, extra_c))
        qaug_o[0, 0, 0, rs, LANES:2 * LANES] = jnp.where(lane < NSA_D, x * (scale * LOG2E), extra).astype(BF16)
        qc_scr[rs, 0:LANES] = jnp.where(lane < NSA_D, x, extra_c).astype(BF16)
        qc_scr[rs, LANES:2 * LANES] = jnp.where(lane < NSA_D, x, 0.0).astype(BF16)

    s_scr[...] = _dot_nt(kcat_ref[0, 0], qc_scr[...])
    cmp_end = lax.broadcasted_iota(jnp.int32, (ncp, tq), 0) * CMP_STRIDE + (CMP_BLOCK - 1)
    t_c = q0 + lax.broadcasted_iota(jnp.int32, (ncp, tq), 1)
    mask_c = cmp_end <= t_c
    psum = jnp.zeros((ncp, tq), F32)
    inv = []
    for h in range(NSA_HPG):
        cs = slice(tq * h, tq * (h + 1))
        s = jnp.where(mask_c, s_scr[:, cs], -jnp.inf)
        m = jnp.max(s, axis=0, keepdims=True)
        m = jnp.where(m == -jnp.inf, 0.0, m)
        p = jnp.exp2(s - m)
        r = 1.0 / jnp.maximum(jnp.sum(p, axis=0, keepdims=True), 1.0)
        p_scr[:, cs] = p.astype(BF16)
        psum = psum + p * r
        inv.append(r)
    oc_t = _dot(vct_ref[0, 0], p_scr[...])
    oc_o[0] = jnp.concatenate([oc_t[:, tq * h:tq * (h + 1)] * inv[h] for h in range(NSA_HPG)], axis=0).T

    ratio = SLC_BLOCK // CMP_STRIDE
    r = CMP_BLOCK // CMP_STRIDE
    bb = lax.broadcasted_iota(jnp.int32, (LANES, ncp), 0)
    mm = lax.broadcasted_iota(jnp.int32, (LANES, ncp), 1)
    a_t = jnp.where((mm >= ratio * bb - (r - 1)) & (mm <= ratio * bb + ratio - 1), 1.0, 0.0).astype(BF16)
    p_hi = psum.astype(BF16)
    p_lo = (psum - p_hi.astype(F32)).astype(BF16)
    imp = _dot(a_t, p_hi) + _dot(a_t, p_lo)

    blk = lax.broadcasted_iota(jnp.int32, (LANES, tq), 0)
    t_s = q0 + lax.broadcasted_iota(jnp.int32, (LANES, tq), 1)
    cur = jnp.right_shift(t_s, SLC_SHIFT)
    forced = (blk == 0) | (blk == cur) | (blk == cur - 1)
    valid = blk * SLC_BLOCK <= t_s
    cand = jnp.where(forced, jnp.inf, jnp.where(valid, imp, -jnp.inf))
    blkf = blk.astype(F32)

    def pick_one(_, carry):
        cand, sel = carry
        m = jnp.max(cand, axis=0, keepdims=True)
        idx = jnp.min(jnp.where(cand == m, blkf, float(LANES)), axis=0, keepdims=True)
        pick = blkf == idx
        return jnp.where(pick, -jnp.inf, cand), jnp.where(pick, 1.0, sel)

    _, sel = lax.fori_loop(0, n_top, pick_one, (cand, jnp.zeros((LANES, tq), F32)))
    pen_t = jnp.where(valid, jnp.where(sel > 0.0, 0.0, NEG), NEG)
    pen = pen_t.T
    for h in range(NSA_HPG):
        qaug_o[0, 0, 0, tq * h:tq * (h + 1), 0:LANES] = pen.astype(BF16)
    used_o[0, 0, 0] = jnp.where(jnp.max(pen, axis=0, keepdims=True) == 0.0, 1.0, 0.0)


def _nsa_select(qa, kc, vct, *, n_top):
    B, S, _ = qa.shape
    G = NSA_GROUPS
    tq = Q_BLOCK
    nq = S // tq
    ncp = kc.shape[2]
    gw = NSA_HPG * NSA_D
    return pl.pallas_call(
        functools.partial(_nsa_select_kernel, n_top=n_top),
        grid=(B, G, nq),
        in_specs=[pl.BlockSpec((1, tq, gw), lambda b, g, i: (b, i, g)),
                  pl.BlockSpec((1, 1, ncp, 2 * LANES), lambda b, g, i: (b, g, 0, 0)),
                  pl.BlockSpec((1, 1, NSA_D, ncp), lambda b, g, i: (b, g, 0, 0))],
        out_specs=(pl.BlockSpec((1, tq, gw), lambda b, g, i: (b, i, g)),
                   pl.BlockSpec((1, 1, 1, NSA_HPG * tq, 2 * LANES), lambda b, g, i: (b, g, i, 0, 0)),
                   pl.BlockSpec((1, 1, 1, 1, LANES), lambda b, g, i: (b, g, i, 0, 0))),
        out_shape=(jax.ShapeDtypeStruct((B, S, G * gw), F32),
                   jax.ShapeDtypeStruct((B, G, nq, NSA_HPG * tq, 2 * LANES), BF16),
                   jax.ShapeDtypeStruct((B, G, nq, 1, LANES), F32)),
        scratch_shapes=[pltpu.VMEM((NSA_HPG * tq, 2 * LANES), BF16),
                        pltpu.VMEM((ncp, NSA_HPG * tq), F32), pltpu.VMEM((ncp, NSA_HPG * tq), BF16)],
        compiler_params=_params("parallel", "parallel", "parallel"),
        name="nsa_select",
    )(qa, kc, vct)


CHUNK = 64


def _scores(q, k, buf, row0=0):
    buf[0][row0:, :k.shape[0]] = _dot_nt(q[row0:], k)


def _softmax_pv(buf, v, m_scr, acc_scr, bias_fn=None, bias_key=None, row0=0):
    s_scr, p_scr = buf
    rows, n = m_scr.shape[0], v.shape[0]
    biases = {}
    for r0 in range(row0, rows, CHUNK):
        rs = slice(r0, r0 + CHUNK)
        s = s_scr[rs, :n]
        if bias_fn is not None:
            key = r0 if bias_key is None else bias_key(r0)
            if key not in biases:
                biases[key] = bias_fn(r0)
            if biases[key] is not None:
                s = s + biases[key]
        m_old = m_scr[rs]
        m_new = jnp.maximum(m_old, jnp.broadcast_to(jnp.max(s, axis=-1, keepdims=True), m_old.shape))
        m_scr[rs] = m_new
        alpha = jnp.exp2(m_old - m_new)
        for c in range(0, acc_scr.shape[1], LANES):
            acc_scr[rs, c:c + LANES] = acc_scr[rs, c:c + LANES] * alpha
        for c in range(0, n, LANES):
            p_scr[rs, c:c + LANES] = jnp.exp2((s[:, c:c + LANES] - m_new).astype(BF16))
    acc_scr[row0:] += _dot(p_scr[row0:, :n], v)


def _causal_flash(q, k_ref, v_ref, n_full, tk, buf_a, buf_b, m_scr, acc_scr, tail, bias_key=None,
                  overlap_fn=None, n_full_even=False, tile_at=lambda p: p):
    k_tile = lambda p: k_ref[pl.ds(pl.multiple_of(tile_at(p) * tk, tk), tk), :]
    v_tile = lambda p: v_ref[pl.ds(pl.multiple_of(tile_at(p) * tk, tk), tk), :]
    n_pairs = lax.div(n_full, 2)

    def finish(j, steps):
        bufs = (buf_a, buf_b)
        for k, (row0, bias_fn) in enumerate(steps):
            if k + 1 < len(steps):
                _scores(q, k_tile(j + k + 1), bufs[(k + 1) % 2], row0=steps[k + 1][0])
            _softmax_pv(bufs[k % 2], v_tile(j + k), m_scr, acc_scr, bias_fn=bias_fn, bias_key=bias_key, row0=row0)

    _scores(q, k_tile(0), buf_a)
    if overlap_fn is not None:
        overlap_fn()

    def pair(jj, carry):
        j = 2 * jj
        _scores(q, k_tile(j + 1), buf_b)
        _softmax_pv(buf_a, v_tile(j), m_scr, acc_scr)
        _scores(q, k_tile(j + 2), buf_a)
        _softmax_pv(buf_b, v_tile(j + 1), m_scr, acc_scr)
        return carry

    lax.fori_loop(0, n_pairs, pair, 0)
    j = 2 * n_pairs
    if n_full_even:
        finish(j, tail)
    else:
        pl.when(j < n_full)(lambda: finish(j, [(0, None)] + tail))
        pl.when(j == n_full)(lambda: finish(j, tail))


def _causal_bias(t0, pos0, n):
    t = t0 + lax.broadcasted_iota(jnp.int32, (CHUNK, n), 0)
    pos = pos0 + lax.broadcasted_iota(jnp.int32, (CHUNK, n), 1)
    return jnp.where(pos <= t, 0.0, NEG)


def _nsa_attend_kernel(used_ref, qaug_ref, ks_ref, vs_ref, kw_ref, vw_ref, oc_ref, gs_ref, o_ref,
                       s_a, s_b, p_a, p_b, m_scr, mw_scr, accs_scr, accw_scr, tiles_ref, *, tk, n_tiles):
    i = pl.program_id(2)
    step = (pl.program_id(0) * pl.num_programs(1) + pl.program_id(1)) * pl.num_programs(2) + i
    buf_a, buf_b = (s_a, p_a), (s_b, p_b)
    qb = qaug_ref.shape[2]
    blk_rows = NSA_HPG * Q_BLOCK
    rows = qb * blk_rows
    q = qaug_ref[0, 0].reshape(rows, 2 * LANES)
    q0 = i * (qb * Q_BLOCK)
    row_dt = lambda r0: (r0 // blk_rows) * Q_BLOCK + r0 % Q_BLOCK

    offw = jnp.maximum(q0 - WINDOW, 0)
    w_off = lambda w: pl.multiple_of(offw + w * tk, Q_BLOCK)

    def window_bias(w):
        def bias(r0):
            d = (q0 + row_dt(r0) + lax.broadcasted_iota(jnp.int32, (CHUNK, tk), 0)
                 - (w_off(w) + lax.broadcasted_iota(jnp.int32, (CHUNK, tk), 1)))
            return jnp.where(d >= 0, jnp.where(d < WINDOW, 0.0, NEG), NEG)
        return bias

    def window_softmax(w, buf):
        _softmax_pv(buf, vw_ref[0, 0, pl.ds(w_off(w), tk), :], mw_scr, accw_scr,
                    bias_fn=window_bias(w), bias_key=row_dt)

    mw_scr[...] = jnp.full((rows, LANES), NEG, F32)
    accw_scr[...] = jnp.zeros(accw_scr.shape, F32)
    _scores(q, kw_ref[0, 0, pl.ds(w_off(0), tk), :], buf_a)
    _scores(q, kw_ref[0, 0, pl.ds(w_off(1), tk), :], buf_b)
    window_softmax(0, buf_a)

    m_scr[...] = jnp.full((rows, LANES), NEG, F32)
    accs_scr[...] = jnp.zeros(accs_scr.shape, F32)
    n_full = lax.div(q0, tk)
    n_used = jnp.int32(0)
    for t in range(n_tiles):
        tiles_ref[n_used] = t
        n_used = n_used + jnp.where((used_ref[step * n_tiles + t] > 0) & (t < n_full), 1, 0)
    tile_at = lambda p: jnp.where(p < n_used, tiles_ref[jnp.minimum(p, n_tiles)], n_full + p - n_used)
    _causal_flash(q, ks_ref.at[0, 0], vs_ref.at[0, 0], n_used, tk, buf_a, buf_b, m_scr, accs_scr,
                  tail=[(0, lambda r0: _causal_bias(q0 + row_dt(r0), n_full * tk, tk))], bias_key=row_dt,
                  overlap_fn=lambda: window_softmax(1, buf_b), tile_at=tile_at)

    gw = NSA_HPG * NSA_D
    gs = gs_ref[0, 0]
    src = lax.broadcasted_iota(jnp.int32, (LANES, 3 * gw), 0)
    dst = lax.broadcasted_iota(jnp.int32, (LANES, 3 * gw), 1)
    branch = jnp.right_shift(dst, gw.bit_length() - 1)
    head = jnp.right_shift(dst & (gw - 1), NSA_D.bit_length() - 1)
    spread = jnp.where(src == 3 * head + branch, 1.0, 0.0).astype(BF16)
    gs_hi = gs.astype(BF16)
    gs_lo = (gs - gs_hi.astype(F32)).astype(BF16)
    g_exp = _dot(gs_hi, spread) + _dot(gs_lo, spread)

    lane = lax.broadcasted_iota(jnp.int32, (Q_BLOCK, LANES), 1)

    def normalised(acc_ref, c, h2):
        r0 = blk_rows * c + Q_BLOCK * 2 * h2
        even = acc_ref[r0:r0 + Q_BLOCK, :]
        odd = acc_ref[r0 + Q_BLOCK:r0 + 2 * Q_BLOCK, :]
        return jnp.where(lane < NSA_D, even[:, :LANES] / even[:, LANES:], odd[:, LANES:] / odd[:, :LANES])

    for c in range(qb):
        qs = slice(Q_BLOCK * c, Q_BLOCK * (c + 1))
        for h2 in range(NSA_HPG // 2):
            ls = slice(LANES * h2, LANES * (h2 + 1))
            gate = lambda j: g_exp[qs, gw * j + LANES * h2:gw * j + LANES * (h2 + 1)]
            o = (gate(0) * oc_ref[0, qs, ls] + gate(1) * normalised(accs_scr, c, h2)
                 + gate(2) * normalised(accw_scr, c, h2))
            o_ref[0, qs, ls] = o.astype(BF16)


def _nsa_attend(qaug, used, ks, vs, kw, vw, oc, gs, *, tk=512, qb=4):
    B, G, nq = qaug.shape[:3]
    S = ks.shape[2]
    tq = qb * Q_BLOCK
    tk = min(tk, S)
    gw = NSA_HPG * NSA_D
    rows = NSA_HPG * tq
    nw = WINDOW + tq
    assert S >= nw and nq % qb == 0 and tk % tq == 0 and nw == 2 * tk
    kv = lambda w: pl.BlockSpec((1, 1, S, w), lambda b, g, i, u: (b, g, 0, 0), pipeline_mode=pl.Buffered(1))
    n_tiles = LANES * SLC_BLOCK // tk
    used = used.reshape(B, G, nq // qb, qb, n_tiles, tk // SLC_BLOCK).max(axis=(3, 5))
    used = (used > 0).astype(jnp.int32).reshape(-1)
    return pl.pallas_call(
        functools.partial(_nsa_attend_kernel, tk=tk, n_tiles=n_tiles),
        grid_spec=pltpu.PrefetchScalarGridSpec(
            num_scalar_prefetch=1,
            grid=(B, G, nq // qb),
            in_specs=[pl.BlockSpec((1, 1, qb, NSA_HPG * Q_BLOCK, 2 * LANES), lambda b, g, i, u: (b, g, i, 0, 0)),
                      kv(2 * LANES), kv(2 * LANES), kv(2 * LANES), kv(2 * LANES),
                      pl.BlockSpec((1, tq, gw), lambda b, g, i, u: (b, i, g)),
                      pl.BlockSpec((1, 1, tq, LANES), lambda b, g, i, u: (b, g, i, 0))],
            out_specs=pl.BlockSpec((1, tq, gw), lambda b, g, i, u: (b, i, g)),
            scratch_shapes=[pltpu.VMEM((rows, tk), F32), pltpu.VMEM((rows, tk), F32),
                            pltpu.VMEM((rows, tk), BF16), pltpu.VMEM((rows, tk), BF16),
                            pltpu.VMEM((rows, LANES), F32), pltpu.VMEM((rows, LANES), F32),
                            pltpu.VMEM((rows, 2 * LANES), F32), pltpu.VMEM((rows, 2 * LANES), F32),
                            pltpu.SMEM((n_tiles + 1,), jnp.int32)]),
        out_shape=jax.ShapeDtypeStruct((B, S, G * gw), BF16),
        compiler_params=_params("parallel", "parallel", "arbitrary"),
        name="nsa_attend",
    )(used, qaug, ks, vs, kw, vw, oc, gs)


def _mla_kernel(q_ref, k_ref, v_ref, o_ref, s_a, s_b, p_a, p_b, m_scr, acc_scr, *, tk):
    i = pl.program_id(2)
    q = q_ref[0]
    m_scr[...] = jnp.full(m_scr.shape, NEG, F32)
    acc_scr[...] = jnp.zeros(acc_scr.shape, F32)
    tail = [(0, lambda r0: _causal_bias(r0, 0, tk) if r0 < tk else None),
            (tk, lambda r0: _causal_bias(r0 - tk, 0, tk))]
    _causal_flash(q, k_ref.at[0], v_ref.at[0], 2 * i, tk, (s_a, p_a), (s_b, p_b), m_scr, acc_scr,
                  tail=tail, n_full_even=True)
    acc = acc_scr[...]
    o_ref[0] = (acc / acc[:, MLA_V:MLA_V + 1]).astype(BF16)


def _mla_attend(qcat, kcat, vcat, *, tk=512):
    B, S, _ = qcat.shape
    tk = min(tk, S // 2)
    t = 2 * tk
    H = MLA_HEADS
    kv = pl.BlockSpec((1, S, LANES), lambda b, h, i: (b, 0, h))
    return pl.pallas_call(
        functools.partial(_mla_kernel, tk=tk),
        grid=(B, H, S // t),
        in_specs=[pl.BlockSpec((1, t, LANES), lambda b, h, i: (b, i, h)), kv, kv],
        out_specs=pl.BlockSpec((1, t, LANES), lambda b, h, i: (b, i, h)),
        out_shape=jax.ShapeDtypeStruct((B, S, H * LANES), BF16),
        scratch_shapes=[pltpu.VMEM((t, tk), F32), pltpu.VMEM((t, tk), F32),
                        pltpu.VMEM((t, tk), BF16), pltpu.VMEM((t, tk), BF16),
                        pltpu.VMEM((t, LANES), F32), pltpu.VMEM((t, LANES), F32)],
        compiler_params=_params("parallel", "parallel", "arbitrary"),
        name="mla_attend",
    )(qcat, kcat, vcat)


def _merge_kernel(x_ref, g_ref, gm_ref, oa_ref, ob_ref, wa_ref, wb_ref, wo_ref, o_ref):
    D = x_ref.shape[2]
    gm = gm_ref[0]
    y = gm[:, :D] * _dot(oa_ref[0], wa_ref[...]) + gm[:, D:] * _dot(ob_ref[0], wb_ref[...])
    o_ref[0] = x_ref[0] + g_ref[0] * _dot(y.astype(BF16), wo_ref[...])


def _merge(x, g, gm, oa, ob, wa, wb, wo, *, tm=512):
    B, S, D = x.shape
    row = lambda w: pl.BlockSpec((1, tm, w), lambda b, i: (b, i, 0))
    res = lambda a: _resident(a.shape, lambda b, i: (0, 0))
    return pl.pallas_call(
        _merge_kernel,
        grid=(B, S // tm),
        in_specs=[row(D), pl.BlockSpec((1, 1, D), lambda b, i: (b, 0, 0)), row(2 * D),
                  row(oa.shape[2]), row(ob.shape[2]), res(wa), res(wb), res(wo)],
        out_specs=row(D),
        out_shape=jax.ShapeDtypeStruct((B, S, D), F32),
        compiler_params=_params("parallel", "parallel"),
        name="merge",
    )(x, g, gm, oa, ob, wa, wb, wo)


def _pad_heads(w, n_heads, width):
    k = w.shape[0]
    w = w.reshape(k, n_heads, width)
    return jnp.pad(w, ((0, 0), (0, 0), (0, LANES - width))).reshape(k, n_heads * LANES)


def _rot_cols(w):
    half = w.shape[-1] // 2
    return jnp.concatenate([-w[..., half:], w[..., :half]], axis=-1)


def _prep_inproj_weights(w_in, w_uq, w_uk, w_uv):
    D = w_in.shape[0]
    o = 0
    seg = {}
    for name, width in (("qa", 512), ("kc", 128), ("vc", 128), ("ks", 128), ("vs", 128), ("kw", 128),
                        ("vw", 128), ("gnsa", 3 * NSA_HEADS), ("cq", MLA_Q_RANK), ("ckv", MLA_KV_RANK),
                        ("kr", MLA_ROPE), ("gm", 2048)):
        seg[name] = w_in[:, o:o + width]
        o += width
    per_g = 3 * NSA_HPG
    gn = jnp.concatenate(
        [jnp.pad(seg["gnsa"][:, per_g * g:per_g * (g + 1)], ((0, 0), (0, LANES - per_g)))
         for g in range(NSA_GROUPS)], axis=1)
    z64 = jnp.zeros((D, MLA_NOPE), F32)
    z32 = jnp.zeros((D, LANES - MLA_NOPE - MLA_ROPE), F32)
    kr = jnp.concatenate([z64, seg["kr"], z32, z64, _rot_cols(seg["kr"]), z32], axis=1)
    w_all = jnp.concatenate([seg["qa"], seg["kc"], seg["vc"], seg["ks"], seg["vs"], seg["kw"], seg["vw"],
                             gn, seg["cq"], seg["ckv"], kr, seg["gm"]], axis=1).astype(BF16)
    assert w_all.shape[1] == _SEG_TOTAL
    hd = MLA_NOPE + MLA_ROPE
    uq = w_uq.reshape(-1, MLA_HEADS, hd)
    uq_rot = jnp.concatenate([jnp.zeros_like(uq[..., :MLA_NOPE]), _rot_cols(uq[..., MLA_NOPE:])], axis=-1)
    wqa = _pad_heads(w_uq, MLA_HEADS, hd).astype(BF16)
    wqb = _pad_heads(uq_rot.reshape(-1, MLA_HEADS * hd), MLA_HEADS, hd).astype(BF16)
    wuk = _pad_heads(w_uk, MLA_HEADS, MLA_NOPE).astype(BF16)
    wuv = _pad_heads(w_uv, MLA_HEADS, MLA_V).astype(BF16)
    return w_all, wqa, wqb, wuk, wuv


def _prep_compress_weights(k_w1, k_w2, k_pe, v_w1, v_w2, v_pe):
    d = NSA_D
    half = CMP_STRIDE

    def blockdiag(mats):
        n = len(mats)
        z = jnp.zeros_like(mats[0])
        rows = [jnp.concatenate([mats[r] if c == r else z for c in range(n)], axis=-1) for r in range(n)]
        return jnp.concatenate(rows, axis=-2)

    def first(w1, lo):
        return w1.reshape(CMP_BLOCK, d, d)[lo:lo + half]

    def w1_half(lo):
        k, v = first(k_w1, lo), first(v_w1, lo)
        return blockdiag([k, k, v, v]).reshape(half * 4 * d, 4 * d).astype(BF16)

    def pe_half(lo):
        k, v = k_pe[lo:lo + half], v_pe[lo:lo + half]
        return jnp.concatenate([k, k, v, v], axis=-1).reshape(1, half * 4 * d)

    w2 = blockdiag([k_w2, k_w2, v_w2, v_w2]).astype(BF16)
    return pe_half(0), pe_half(half), w1_half(0), w1_half(half), w2


def _rope_tables(S):
    half = MLA_ROPE // 2
    inv_freq = ROPE_THETA ** (-jnp.arange(half, dtype=F32) / half)
    ang = jnp.arange(S, dtype=F32)[:, None] * inv_freq[None, :]
    cos, sin = jnp.cos(ang), jnp.sin(ang)
    ones = jnp.ones((S, MLA_NOPE), F32)
    z_n = jnp.zeros((S, MLA_NOPE), F32)
    z_p = jnp.zeros((S, LANES - MLA_NOPE - MLA_ROPE), F32)
    ctab = jnp.concatenate([ones, cos, cos, z_p], axis=1)
    stab = jnp.concatenate([z_n, sin, sin, z_p], axis=1)
    return ctab, stab


def kernel(x, c, w_ada, b_ada, ffn1_gate, ffn1_up, ffn1_down, ffn2_gate, ffn2_up, ffn2_down, w_in, cmpk_w1, cmpk_w2, cmpk_pe, cmpv_w1, cmpv_w2, cmpv_pe, mla_q_norm, mla_w_uq, mla_kv_norm, mla_w_uk, mla_w_uv, w_branch_a, w_branch_b, w_out, final_norm):
    B, S, D = x.shape
    L = w_ada.shape[0]
    assert NSA_GROUPS == 2 and S // SLC_BLOCK <= LANES and S // (2 * CMP_STRIDE) <= 256 and S % 512 == 0
    n_top = min(N_SELECT, S // SLC_BLOCK)
    ctab, stab = _rope_tables(S)
    mod = _adaln(c, w_ada, b_ada).reshape(L, B, N_MOD, 1, D)
    bf = lambda w: w.astype(BF16)
    for l in range(L):
        sh1, sc1, g1, sh2, sc2, g2, sh3, sc3, g3 = (mod[l, :, k] for k in range(N_MOD))
        x = _ffn(x, sh1, sc1, g1, bf(ffn1_gate[l]), bf(ffn1_up[l]), bf(ffn1_down[l]), final_norm, final=False)

        w_all, wqa, wqb, wuk, wuv = _prep_inproj_weights(w_in[l], mla_w_uq[l], mla_w_uk[l], mla_w_uv[l])
        (qa, kcvc, ks, vs, kw, vw, gs, qcat, kcat, vcat, gm) = _inproj(
            x, sh2, sc2, w_all, wqa, wqb, wuk, wuv,
            mla_q_norm[l].reshape(1, -1), mla_kv_norm[l].reshape(1, -1), ctab, stab)
        kc, vct = _compress(kcvc, *_prep_compress_weights(
            cmpk_w1[l], cmpk_w2[l], cmpk_pe[l], cmpv_w1[l], cmpv_w2[l], cmpv_pe[l]))
        oc, qaug, used = _nsa_select(qa, kc, vct, n_top=n_top)
        oa = _nsa_attend(qaug, used, ks, vs, kw, vw, oc, gs)
        ob = _mla_attend(qcat, kcat, vcat)
        wb_pad = jnp.pad(w_branch_b[l].reshape(MLA_HEADS, MLA_V, D),
                         ((0, 0), (0, LANES - MLA_V), (0, 0))).reshape(MLA_HEADS * LANES, D)
        x = _merge(x, g2, gm, oa, ob, bf(w_branch_a[l]), bf(wb_pad), bf(w_out[l]))

        x = _ffn(x, sh3, sc3, g3, bf(ffn2_gate[l]), bf(ffn2_up[l]), bf(ffn2_down[l]), final_norm,
                 final=(l == L - 1))
    return x
```

```python
import functools
import math

import ml_dtypes
import numpy as np
import jax
import jax.numpy as jnp
from jax import lax
from jax.experimental import pallas as pl
from jax.experimental.pallas import tpu as pltpu

F32 = jnp.float32
BF16 = jnp.bfloat16

NSA_HEADS = 8
NSA_GROUPS = 2
NSA_HPG = NSA_HEADS // NSA_GROUPS
NSA_D = 64
CMP_BLOCK = 32
CMP_STRIDE = 16
SLC_BLOCK = 64
N_SELECT = 16
WINDOW = 512
MLA_HEADS = 8
MLA_NOPE = 64
MLA_ROPE = 32
MLA_V = 64
MLA_Q_RANK = 384
MLA_KV_RANK = 256
ROPE_THETA = 10000.0
N_MOD = 9
Q_BLOCK = 128
NORM_EPS = 1e-6

LANES = 128
MXU_WIDTH = 256
SLC_SHIFT = SLC_BLOCK.bit_length() - 1
assert 1 << SLC_SHIFT == SLC_BLOCK
LOG2E = math.log2(math.e)
SLOPES = [2.0 ** (-8.0 * (k + 1) / NSA_HEADS) for k in range(NSA_HEADS)]
N_PIECES = 3


def _bf16_pieces(v):
    out = []
    for _ in range(N_PIECES):
        p = float(np.float32(v).astype(ml_dtypes.bfloat16))
        out.append(p)
        v -= p
    return out


SLOPE_PIECES = [_bf16_pieces(v * LOG2E) for v in SLOPES]
NEG = -1e30
VMEM_LIMIT = 56 * 1024 * 1024


def _rms(xf):
    return xf * lax.rsqrt(jnp.mean(xf * xf, -1, keepdims=True) + NORM_EPS)


def _sigmoid(x):
    return 1.0 / (1.0 + jnp.exp(-x))


def _dot(a, b):
    return jnp.dot(a, b, preferred_element_type=F32)


def _dot_nt(a, b, precision=None):
    return lax.dot_general(a, b, (((1,), (1,)), ((), ())), precision=precision,
                           preferred_element_type=F32)


def _params(*sem):
    return pltpu.CompilerParams(dimension_semantics=sem, vmem_limit_bytes=VMEM_LIMIT)


def _resident(shape, index_map):
    return pl.BlockSpec(shape, index_map, pipeline_mode=pl.Buffered(1))


def _adaln_kernel(c_ref, w_ref, b_ref, o_ref):
    c = c_ref[...]
    a = c * _sigmoid(c)
    o_ref[0] = jnp.dot(a, w_ref[0], precision=lax.Precision.HIGHEST,
                       preferred_element_type=F32) + b_ref[0]


def _adaln(c, w_ada, b_ada):
    L, D, N = w_ada.shape
    B = c.shape[0]
    rows = 8
    c_pad = jnp.zeros((rows, D), F32).at[:B].set(c)
    tn = N // 8
    out = pl.pallas_call(
        _adaln_kernel,
        grid=(L, N // tn),
        in_specs=[pl.BlockSpec((rows, D), lambda l, j: (0, 0)),
                  pl.BlockSpec((1, D, tn), lambda l, j: (l, 0, j)),
                  pl.BlockSpec((1, 1, tn), lambda l, j: (l, 0, j))],
        out_specs=pl.BlockSpec((1, rows, tn), lambda l, j: (l, 0, j)),
        out_shape=jax.ShapeDtypeStruct((L, rows, N), F32),
        compiler_params=_params("parallel", "parallel"),
        name="adaln_mod",
    )(c_pad, w_ada, b_ada.reshape(L, 1, N))
    return out[:, :B]


def _ffn_kernel(x_ref, sh_ref, sc_ref, g_ref, wg_ref, wu_ref, wd_ref, fn_ref, o_ref, *, splits, final):
    x = x_ref[0]
    u = (_rms(x) * (1.0 + sc_ref[0]) + sh_ref[0]).astype(BF16)
    acc = jnp.zeros(x.shape, F32)
    for f0, f1 in zip(splits[:-1], splits[1:]):
        hg = _dot(u, wg_ref[:, f0:f1])
        hu = _dot(u, wu_ref[:, f0:f1])
        a = (hg * _sigmoid(hg) * hu).astype(BF16)
        acc = acc + _dot(a, wd_ref[f0:f1, :])
    y = x + (0.5 * g_ref[0]) * acc
    if final:
        y = _rms(y) * fn_ref[...]
    o_ref[0] = y


def _cast_kernel(x_ref, o_ref):
    o_ref[...] = x_ref[...].astype(o_ref.dtype)


def _to_bf16(w, *, tr=256):
    L, R, C = w.shape
    spec = pl.BlockSpec((1, tr, C), lambda l, i: (l, i, 0))
    return pl.pallas_call(
        _cast_kernel, grid=(L, R // tr), in_specs=[spec], out_specs=spec,
        out_shape=jax.ShapeDtypeStruct(w.shape, BF16),
        compiler_params=_params("parallel", "parallel"), name="to_bf16",
    )(w)


def _ffn(x, sh, sc, g, wg, wu, wd, layer, final_norm, *, final, tm=512):
    B, S, D = x.shape
    n_ff = wg.shape[2]
    cut = (n_ff // 2 + MXU_WIDTH - 1) // MXU_WIDTH * MXU_WIDTH
    splits = (0, cut, n_ff) if 0 < cut < n_ff else (0, n_ff)
    vec = pl.BlockSpec((1, 1, D), lambda b, i: (b, 0, 0))
    return pl.pallas_call(
        functools.partial(_ffn_kernel, splits=splits, final=final),
        grid=(B, S // tm),
        in_specs=[pl.BlockSpec((1, tm, D), lambda b, i: (b, i, 0)), vec, vec, vec,
                  _resident((None, D, n_ff), lambda b, i: (layer, 0, 0)),
                  _resident((None, D, n_ff), lambda b, i: (layer, 0, 0)),
                  _resident((None, n_ff, D), lambda b, i: (layer, 0, 0)),
                  pl.BlockSpec((1, D), lambda b, i: (0, 0))],
        out_specs=pl.BlockSpec((1, tm, D), lambda b, i: (b, i, 0)),
        out_shape=jax.ShapeDtypeStruct((B, S, D), F32),
        compiler_params=_params("parallel", "parallel"),
        name="ffn_final" if final else "ffn",
    )(x, sh, sc, g, wg, wu, wd, final_norm.reshape(1, D))


_SEG = {}
_o = 0
for _name, _w in (("qa", 512), ("kcvc", 256), ("kv4", 512), ("gnsa", 256), ("cq", MLA_Q_RANK),
                  ("ckv", MLA_KV_RANK), ("kr", 256), ("gm", 2048)):
    _SEG[_name] = (_o, _o + _w)
    _o += _w
_SEG_TOTAL = _o


def _inproj_kernel(x_ref, sh_ref, sc_ref, w_ref, wqa_ref, wqb_ref, wuk_ref, wuv_ref, qn_ref, kvn_ref,
                   ctab_ref, stab_ref,
                   qa_o, kcvc_o, ks_o, vs_o, kw_o, vw_o, gs_o, qcat_o, kcat_o, vcat_o, gm_o, *, tm):
    i = pl.program_id(1)
    x = x_ref[0]
    u = (_rms(x) * (1.0 + sc_ref[0]) + sh_ref[0]).astype(BF16)

    def proj(name):
        a, b = _SEG[name]
        return _dot(u, w_ref[:, a:b])

    qa_o[0] = proj("qa").astype(BF16)
    kcvc = proj("kcvc")
    kcvc_o[0, 0] = kcvc[:, :LANES]
    kcvc_o[0, 1] = kcvc[:, LANES:]

    kv = proj("kv4")
    lane = lax.broadcasted_iota(jnp.int32, (tm, LANES), 1)
    pos = i * tm + lax.broadcasted_iota(jnp.int32, (tm, LANES), 0)
    blk = jnp.right_shift(pos, SLC_SHIFT)
    rem = pos - blk * SLC_BLOCK
    onehot = jnp.where(lane == blk, 1.0, 0.0).astype(BF16)
    zeros = jnp.zeros((tm, LANES), BF16)
    kextra = jnp.where(lane < NSA_D + N_PIECES, rem.astype(F32),
                       jnp.where(lane < NSA_D + 2 * N_PIECES, blk.astype(F32), 0.0))
    for g in range(NSA_GROUPS):
        for src, k_o, v_o, first in ((0, ks_o, vs_o, onehot), (2, kw_o, vw_o, zeros)):
            kk = kv[:, LANES * src:LANES * (src + 1)]
            vv = kv[:, LANES * (src + 1):LANES * (src + 2)]
            vr = pltpu.roll(vv, NSA_D, 1)
            v_lo, v_hi = (vv, vr) if g == 0 else (vr, vv)
            if g == 1:
                kk = pltpu.roll(kk, NSA_D, 1)
            k_o[0, g, :, 0:LANES] = first
            k_o[0, g, :, LANES:2 * LANES] = jnp.where(lane < NSA_D, kk, kextra).astype(BF16)
            v_o[0, g, :, 0:LANES] = jnp.where(lane < NSA_D, v_lo, 1.0).astype(BF16)
            v_o[0, g, :, LANES:2 * LANES] = jnp.where(lane < NSA_D, 1.0, v_hi).astype(BF16)

    gs = _sigmoid(proj("gnsa"))
    for g in range(NSA_GROUPS):
        gs_o[0, g] = gs[:, LANES * g:LANES * (g + 1)]

    ctab = ctab_ref[...]
    stab = stab_ref[...]
    scale = (MLA_NOPE + MLA_ROPE) ** -0.5 * LOG2E
    cqn =(_rms(proj("cq")) * qn_ref[...]).astype(BF16)
    q_a = _dot(cqn, wqa_ref[...])
    q_b = _dot(cqn, wqb_ref[...])
    ckvn = (_rms(proj("ckv")) * kvn_ref[...]).astype(BF16)
    kn = _dot(ckvn, wuk_ref[...])
    vv = _dot(ckvn, wuv_ref[...])
    kr = proj("kr")
    krope = kr[:, :LANES] * ctab + kr[:, LANES:] * stab
    for h in range(MLA_HEADS):
        sl = slice(LANES * h, LANES * (h + 1))
        qcat_o[0, :, sl] = ((q_a[:, sl] * ctab + q_b[:, sl] * stab) * scale).astype(BF16)
        kcat_o[0, :, sl] = (kn[:, sl] + krope).astype(BF16)
        vcat_o[0, :, sl] = jnp.where(lane == MLA_V, 1.0, vv[:, sl]).astype(BF16)

    gm_o[0] = _sigmoid(proj("gm"))


def _inproj(x, sh, sc, w_all, wqa, wqb, wuk, wuv, qn, kvn, ctab, stab, *, tm=512):
    B, S, D = x.shape
    G = NSA_GROUPS
    grid = (B, S // tm)
    vec = pl.BlockSpec((1, 1, D), lambda b, i: (b, 0, 0))

    def res(a):
        return _resident(a.shape, lambda b, i: (0,) * a.ndim)

    row = lambda w: pl.BlockSpec((1, tm, w), lambda b, i: (b, i, 0))
    grp = lambda w: pl.BlockSpec((1, G, tm, w), lambda b, i: (b, 0, i, 0))
    tab = pl.BlockSpec((tm, LANES), lambda b, i: (i, 0))
    out_shape = (
        jax.ShapeDtypeStruct((B, S, 512), BF16),
        jax.ShapeDtypeStruct((B, 2, S, LANES), F32),
        jax.ShapeDtypeStruct((B, G, S, 256), BF16),
        jax.ShapeDtypeStruct((B, G, S, 256), BF16),
        jax.ShapeDtypeStruct((B, G, S, 256), BF16),
        jax.ShapeDtypeStruct((B, G, S, 256), BF16),
        jax.ShapeDtypeStruct((B, G, S, 128), F32),
        jax.ShapeDtypeStruct((B, S, MLA_HEADS * LANES), BF16),
        jax.ShapeDtypeStruct((B, S, MLA_HEADS * LANES), BF16),
        jax.ShapeDtypeStruct((B, S, MLA_HEADS * LANES), BF16),
        jax.ShapeDtypeStruct((B, S, 2048), F32),
    )
    out_specs = (row(512), grp(LANES), grp(256), grp(256), grp(256), grp(256), grp(128),
                 row(MLA_HEADS * LANES), row(MLA_HEADS * LANES), row(MLA_HEADS * LANES), row(2048))
    return pl.pallas_call(
        functools.partial(_inproj_kernel, tm=tm),
        grid=grid,
        in_specs=[pl.BlockSpec((1, tm, D), lambda b, i: (b, i, 0)), vec, vec,
                  res(w_all), res(wqa), res(wqb), res(wuk), res(wuv), res(qn), res(kvn), tab, tab],
        out_specs=out_specs,
        out_shape=out_shape,
        compiler_params=_params("parallel", "parallel"),
        name="in_proj",
    )(x, sh, sc, w_all, wqa, wqb, wuk, wuv, qn, kvn, ctab, stab)


def _compress_kernel(a_ref, pea_ref, peb_ref, wa_ref, wb_ref, w2_ref, kcat_o, vct_o):
    planes, s_len, w = a_ref.shape[1:]
    n = s_len // CMP_STRIDE
    pa = jnp.zeros((n, planes * w), F32)
    pb = jnp.zeros((n, planes * w), F32)
    for t in range(CMP_STRIDE):
        a = jnp.concatenate([a_ref[0, j, pl.ds(t, n, stride=CMP_STRIDE), :] for j in range(planes)], axis=1)
        ws = slice(planes * w * t, planes * w * (t + 1))
        pa = pa + _dot((a + pea_ref[:, ws]).astype(BF16), wa_ref[ws, :])
        pb = pb + _dot((a + peb_ref[:, ws]).astype(BF16), wb_ref[ws, :])
    h = pa + pltpu.roll(pb, n - 1, 0)
    h = 0.5 * h * (1.0 + jnp.tanh(0.7978845608028654 * (h + 0.044715 * (h * h * h))))
    out = _dot(h.astype(BF16), w2_ref[...])
    out_t = out.T
    lane = lax.broadcasted_iota(jnp.int32, (n, LANES), 1)
    row = lax.broadcasted_iota(jnp.int32, (n, LANES), 0)
    pair = jnp.right_shift(row, 1)
    within = (row - 2 * pair) * CMP_STRIDE + (CMP_BLOCK - 1)
    kextra = jnp.where(lane < NSA_D + N_PIECES, pair.astype(F32),
                       jnp.where(lane < NSA_D + 2 * N_PIECES, within.astype(F32), 0.0))
    kk = out[:, 0:LANES] * (NSA_D ** -0.5 * LOG2E)
    for g in range(NSA_GROUPS):
        x = kk if g == 0 else pltpu.roll(kk, NSA_D, 1)
        lo = x - x.astype(BF16).astype(F32)
        kcat_o[0, g, :, 0:LANES] = jnp.where(lane < NSA_D, x, kextra).astype(BF16)
        kcat_o[0, g, :, LANES:2 * LANES] = jnp.where(lane < NSA_D, lo, 0.0).astype(BF16)
        vct_o[0, g] = out_t[2 * NSA_D + NSA_D * g:2 * NSA_D + NSA_D * (g + 1), :].astype(BF16)


def _compress(kcvc, pea, peb, wa, wb, w2):
    B, P, S, W = kcvc.shape
    n = S // CMP_STRIDE
    G = NSA_GROUPS
    full = lambda arr: _resident(arr.shape, lambda b: (0,) * arr.ndim)
    return pl.pallas_call(
        _compress_kernel,
        grid=(B,),
        in_specs=[pl.BlockSpec((1, P, S, W), lambda b: (b, 0, 0, 0)),
                  full(pea), full(peb), full(wa), full(wb), full(w2)],
        out_specs=(pl.BlockSpec((1, G, n, 2 * LANES), lambda b: (b, 0, 0, 0)),
                   pl.BlockSpec((1, G, NSA_D, n), lambda b: (b, 0, 0, 0))),
        out_shape=(jax.ShapeDtypeStruct((B, G, n, 2 * LANES), BF16),
                   jax.ShapeDtypeStruct((B, G, NSA_D, n), BF16)),
        compiler_params=_params("parallel"),
        name="compress",
    )(kcvc, pea, peb, wa, wb, w2)


def _nsa_select_kernel(q_ref, kcat_ref, vct_ref, oc_o, qaug_o, used_o, qc_scr, s_scr, p_scr, *, n_top):
    g = pl.program_id(1)
    i = pl.program_id(2)
    tq = q_ref.shape[1]
    ncp = kcat_ref.shape[2]
    q0 = i * tq
    scale = NSA_D ** -0.5

    lane = lax.broadcasted_iota(jnp.int32, (tq, LANES), 1)
    qf = q_ref[0].astype(F32)
    for h in range(NSA_HPG):
        rs = slice(tq * h, tq * (h + 1))
        x = qf[:, LANES * (h // 2):LANES * (h // 2 + 1)]
        if h % 2 == 1:
            x = pltpu.roll(x, NSA_D, 1)
        extra = jnp.zeros((tq, LANES), F32)
        extra_c = jnp.zeros((tq, LANES), F32)
        for k in range(N_PIECES):
            piece = jnp.where(g == 0, SLOPE_PIECES[h][k], SLOPE_PIECES[NSA_HPG + h][k])
            extra = jnp.where(lane == NSA_D + k, piece,
                              jnp.where(lane == NSA_D + N_PIECES + k, piece * SLC_BLOCK, extra))
            extra_c = jnp.where(lane == NSA_D + k, piece * (2 * CMP_STRIDE),
                                jnp.where(lane == NSA_D + N_PIECES + k, piece, extra_c))
        qaug_o[0, 0, 0, rs, LANES:2 * LANES] = jnp.where(lane < NSA_D, x * (scale * LOG2E), extra).astype(BF16)
        qc_scr[rs, 0:LANES] = jnp.where(lane < NSA_D, x, extra_c).astype(BF16)
        qc_scr[rs, LANES:2 * LANES] = jnp.where(lane < NSA_D, x, 0.0).astype(BF16)

    s_scr[...] = _dot_nt(kcat_ref[0, 0], qc_scr[...])
    cmp_end = lax.broadcasted_iota(jnp.int32, (ncp, tq), 0) * CMP_STRIDE + (CMP_BLOCK - 1)
    t_c = q0 + lax.broadcasted_iota(jnp.int32, (ncp, tq), 1)
    mask_c = cmp_end <= t_c
    psum = jnp.zeros((ncp, tq), F32)
    inv = []
    for h in range(NSA_HPG):
        cs = slice(tq * h, tq * (h + 1))
        s = jnp.where(mask_c, s_scr[:, cs], -jnp.inf)
        m = jnp.max(s, axis=0, keepdims=True)
        m = jnp.where(m == -jnp.inf, 0.0, m)
        p = jnp.exp2(s - m)
        r = 1.0 / jnp.maximum(jnp.sum(p, axis=0, keepdims=True), 1.0)
        p_scr[:, cs] = p.astype(BF16)
        psum = psum + p * r
        inv.append(r)
    oc_t = _dot(vct_ref[0, 0], p_scr[...])
    oc_o[0] = jnp.concatenate([oc_t[:, tq * h:tq * (h + 1)] * inv[h] for h in range(NSA_HPG)], axis=0).T

    ratio = SLC_BLOCK // CMP_STRIDE
    r = CMP_BLOCK // CMP_STRIDE
    bb = lax.broadcasted_iota(jnp.int32, (LANES, ncp), 0)
    mm = lax.broadcasted_iota(jnp.int32, (LANES, ncp), 1)
    a_t = jnp.where((mm >= ratio * bb - (r - 1)) & (mm <= ratio * bb + ratio - 1), 1.0, 0.0).astype(BF16)
    p_hi = psum.astype(BF16)
    p_lo = (psum - p_hi.astype(F32)).astype(BF16)
    imp = _dot(a_t, p_hi) + _dot(a_t, p_lo)

    blk = lax.broadcasted_iota(jnp.int32, (LANES, tq), 0)
    t_s = q0 + lax.broadcasted_iota(jnp.int32, (LANES, tq), 1)
    cur = jnp.right_shift(t_s, SLC_SHIFT)
    forced = (blk == 0) | (blk == cur) | (blk == cur - 1)
    valid = blk * SLC_BLOCK <= t_s
    cur_q = jnp.right_shift(q0 + lax.broadcasted_iota(jnp.int32, (1, tq), 1), SLC_SHIFT)
    n_forced = 1 + jnp.where(cur_q >= 1, 1, 0) + jnp.where(cur_q >= 2, 1, 0)
    left = (n_top - n_forced).astype(F32)
    cand = jnp.where(forced, -jnp.inf, jnp.where(valid, imp, -jnp.inf))
    blkf = blk.astype(F32)

    def pick_one(r, carry):
        cand, sel = carry
        m = jnp.max(cand, axis=0, keepdims=True)
        idx = jnp.min(jnp.where(cand == m, blkf, float(LANES)), axis=0, keepdims=True)
        pick = blkf == jnp.where(left > r.astype(F32), idx, -1.0)
        return jnp.where(pick, -jnp.inf, cand), jnp.where(pick, 1.0, sel)

    n_rounds = jnp.where(q0 >= 2 * SLC_BLOCK, n_top - 3, n_top - 1)
    _, sel = lax.fori_loop(0, jnp.maximum(n_rounds, 0), pick_one, (cand, jnp.where(forced, 1.0, 0.0)))
    pen_t = jnp.where(valid, jnp.where(sel > 0.0, 0.0, NEG), NEG)
    pen = pen_t.T
    for h in range(NSA_HPG):
        qaug_o[0, 0, 0, tq * h:tq * (h + 1), 0:LANES] = pen.astype(BF16)
    used_o[0, 0, 0] = jnp.where(jnp.max(pen, axis=0, keepdims=True) == 0.0, 1.0, 0.0)


def _nsa_select(qa, kc, vct, *, n_top):
    B, S, _ = qa.shape
    G = NSA_GROUPS
    tq = Q_BLOCK
    nq = S // tq
    ncp = kc.shape[2]
    gw = NSA_HPG * NSA_D
    return pl.pallas_call(
        functools.partial(_nsa_select_kernel, n_top=n_top),
        grid=(B, G, nq),
        in_specs=[pl.BlockSpec((1, tq, gw), lambda b, g, i: (b, i, g)),
                  pl.BlockSpec((1, 1, ncp, 2 * LANES), lambda b, g, i: (b, g, 0, 0)),
                  pl.BlockSpec((1, 1, NSA_D, ncp), lambda b, g, i: (b, g, 0, 0))],
        out_specs=(pl.BlockSpec((1, tq, gw), lambda b, g, i: (b, i, g)),
                   pl.BlockSpec((1, 1, 1, NSA_HPG * tq, 2 * LANES), lambda b, g, i: (b, g, i, 0, 0)),
                   pl.BlockSpec((1, 1, 1, 1, LANES), lambda b, g, i: (b, g, i, 0, 0))),
        out_shape=(jax.ShapeDtypeStruct((B, S, G * gw), F32),
                   jax.ShapeDtypeStruct((B, G, nq, NSA_HPG * tq, 2 * LANES), BF16),
                   jax.ShapeDtypeStruct((B, G, nq, 1, LANES), F32)),
        scratch_shapes=[pltpu.VMEM((NSA_HPG * tq, 2 * LANES), BF16),
                        pltpu.VMEM((ncp, NSA_HPG * tq), F32), pltpu.VMEM((ncp, NSA_HPG * tq), BF16)],
        compiler_params=_params("parallel", "parallel", "parallel"),
        name="nsa_select",
    )(qa, kc, vct)


CHUNK = 64


def _scores(q, k, buf, row0=0):
    buf[0][row0:, :k.shape[0]] = _dot_nt(q[row0:], k)


def _softmax_pv(buf, v, m_scr, acc_scr, bias_fn=None, bias_key=None, row0=0):
    s_scr, p_scr = buf
    rows, n = m_scr.shape[0], v.shape[0]
    biases = {}
    for r0 in range(row0, rows, CHUNK):
        rs = slice(r0, r0 + CHUNK)
        s = s_scr[rs, :n]
        if bias_fn is not None:
            key = r0 if bias_key is None else bias_key(r0)
            if key not in biases:
                biases[key] = bias_fn(r0)
            if biases[key] is not None:
                s = s + biases[key]
        m_old = m_scr[rs]
        m_new = jnp.maximum(m_old, jnp.broadcast_to(jnp.max(s, axis=-1, keepdims=True), m_old.shape))
        m_scr[rs] = m_new
        alpha = jnp.exp2(m_old - m_new)
        for c in range(0, acc_scr.shape[1], LANES):
            acc_scr[rs, c:c + LANES] = acc_scr[rs, c:c + LANES] * alpha
        for c in range(0, n, LANES):
            p_scr[rs, c:c + LANES] = jnp.exp2((s[:, c:c + LANES] - m_new).astype(BF16))
    acc_scr[row0:] += _dot(p_scr[row0:, :n], v)


def _causal_flash(q, k_ref, v_ref, n_full, tk, buf_a, buf_b, m_scr, acc_scr, tail, bias_key=None,
                  overlap_fn=None, n_full_even=False, tile_at=lambda p: p):
    k_tile = lambda p: k_ref[pl.ds(pl.multiple_of(tile_at(p) * tk, tk), tk), :]
    v_tile = lambda p: v_ref[pl.ds(pl.multiple_of(tile_at(p) * tk, tk), tk), :]
    n_pairs = lax.div(n_full, 2)

    def finish(j, steps):
        bufs = (buf_a, buf_b)
        for k, (row0, bias_fn) in enumerate(steps):
            if k + 1 < len(steps):
                _scores(q, k_tile(j + k + 1), bufs[(k + 1) % 2], row0=steps[k + 1][0])
            _softmax_pv(bufs[k % 2], v_tile(j + k), m_scr, acc_scr, bias_fn=bias_fn, bias_key=bias_key, row0=row0)

    _scores(q, k_tile(0), buf_a)
    if overlap_fn is not None:
        overlap_fn()

    def pair(jj, carry):
        j = 2 * jj
        _scores(q, k_tile(j + 1), buf_b)
        _softmax_pv(buf_a, v_tile(j), m_scr, acc_scr)
        _scores(q, k_tile(j + 2), buf_a)
        _softmax_pv(buf_b, v_tile(j + 1), m_scr, acc_scr)
        return carry

    lax.fori_loop(0, n_pairs, pair, 0)
    j = 2 * n_pairs
    if n_full_even:
        finish(j, tail)
    else:
        pl.when(j < n_full)(lambda: finish(j, [(0, None)] + tail))
        pl.when(j == n_full)(lambda: finish(j, tail))


def _causal_bias(t0, pos0, n):
    t = t0 + lax.broadcasted_iota(jnp.int32, (CHUNK, n), 0)
    pos = pos0 + lax.broadcasted_iota(jnp.int32, (CHUNK, n), 1)
    return jnp.where(pos <= t, 0.0, NEG)


def _nsa_attend_kernel(used_ref, qaug_ref, ks_ref, vs_ref, kw_ref, vw_ref, oc_ref, gs_ref, o_ref,
                       s_a, s_b, p_a, p_b, m_scr, mw_scr, accs_scr, accw_scr, tiles_ref, *, tk, n_tiles):
    i = pl.program_id(2)
    step = (pl.program_id(0) * pl.num_programs(1) + pl.program_id(1)) * pl.num_programs(2) + i
    buf_a, buf_b = (s_a, p_a), (s_b, p_b)
    qb = qaug_ref.shape[2]
    blk_rows = NSA_HPG * Q_BLOCK
    rows = qb * blk_rows
    q = qaug_ref[0, 0].reshape(rows, 2 * LANES)
    q0 = i * (qb * Q_BLOCK)
    row_dt = lambda r0: (r0 // blk_rows) * Q_BLOCK + r0 % Q_BLOCK

    offw = jnp.maximum(q0 - WINDOW, 0)
    w_off = lambda w: pl.multiple_of(offw + w * tk, Q_BLOCK)

    def window_bias(w):
        def bias(r0):
            d = (q0 + row_dt(r0) + lax.broadcasted_iota(jnp.int32, (CHUNK, tk), 0)
                 - (w_off(w) + lax.broadcasted_iota(jnp.int32, (CHUNK, tk), 1)))
            return jnp.where(d >= 0, jnp.where(d < WINDOW, 0.0, NEG), NEG)
        return bias

    def window_softmax(w, buf):
        _softmax_pv(buf, vw_ref[0, 0, pl.ds(w_off(w), tk), :], mw_scr, accw_scr,
                    bias_fn=window_bias(w), bias_key=row_dt)

    mw_scr[...] = jnp.full((rows, LANES), NEG, F32)
    accw_scr[...] = jnp.zeros(accw_scr.shape, F32)
    _scores(q, kw_ref[0, 0, pl.ds(w_off(0), tk), :], buf_a)
    _scores(q, kw_ref[0, 0, pl.ds(w_off(1), tk), :], buf_b)
    window_softmax(0, buf_a)

    m_scr[...] = jnp.full((rows, LANES), NEG, F32)
    accs_scr[...] = jnp.zeros(accs_scr.shape, F32)
    n_full = lax.div(q0, tk)
    n_used = jnp.int32(0)
    for t in range(n_tiles):
        tiles_ref[n_used] = t
        n_used = n_used + jnp.where((used_ref[step * n_tiles + t] > 0) & (t < n_full), 1, 0)
    tile_at = lambda p: jnp.where(p < n_used, tiles_ref[jnp.minimum(p, n_tiles)], n_full + p - n_used)
    _causal_flash(q, ks_ref.at[0, 0], vs_ref.at[0, 0], n_used, tk, buf_a, buf_b, m_scr, accs_scr,
                  tail=[(0, lambda r0: _causal_bias(q0 + row_dt(r0), n_full * tk, tk))], bias_key=row_dt,
                  overlap_fn=lambda: window_softmax(1, buf_b), tile_at=tile_at)

    gw = NSA_HPG * NSA_D
    gs = gs_ref[0, 0]
    src = lax.broadcasted_iota(jnp.int32, (LANES, 3 * gw), 0)
    dst = lax.broadcasted_iota(jnp.int32, (LANES, 3 * gw), 1)
    branch = jnp.right_shift(dst, gw.bit_length() - 1)
    head = jnp.right_shift(dst & (gw - 1), NSA_D.bit_length() - 1)
    spread = jnp.where(src == 3 * head + branch, 1.0, 0.0).astype(BF16)
    gs_hi = gs.astype(BF16)
    gs_lo = (gs - gs_hi.astype(F32)).astype(BF16)
    g_exp = _dot(gs_hi, spread) + _dot(gs_lo, spread)

    lane = lax.broadcasted_iota(jnp.int32, (Q_BLOCK, LANES), 1)

    def normalised(acc_ref, c, h2):
        r0 = blk_rows * c + Q_BLOCK * 2 * h2
        even = acc_ref[r0:r0 + Q_BLOCK, :]
        odd = acc_ref[r0 + Q_BLOCK:r0 + 2 * Q_BLOCK, :]
        return jnp.where(lane < NSA_D, even[:, :LANES] / even[:, LANES:], odd[:, LANES:] / odd[:, :LANES])

    for c in range(qb):
        qs = slice(Q_BLOCK * c, Q_BLOCK * (c + 1))
        for h2 in range(NSA_HPG // 2):
            ls = slice(LANES * h2, LANES * (h2 + 1))
            gate = lambda j: g_exp[qs, gw * j + LANES * h2:gw * j + LANES * (h2 + 1)]
            o = (gate(0) * oc_ref[0, qs, ls] + gate(1) * normalised(accs_scr, c, h2)
                 + gate(2) * normalised(accw_scr, c, h2))
            o_ref[0, qs, ls] = o.astype(BF16)


def _nsa_attend(qaug, used, ks, vs, kw, vw, oc, gs, *, tk=512, qb=4):
    B, G, nq = qaug.shape[:3]
    S = ks.shape[2]
    tq = qb * Q_BLOCK
    tk = min(tk, S)
    gw = NSA_HPG * NSA_D
    rows = NSA_HPG * tq
    nw = WINDOW + tq
    assert S >= nw and nq % qb == 0 and tk % tq == 0 and nw == 2 * tk
    kv = lambda w: pl.BlockSpec((1, 1, S, w), lambda b, g, i, u: (b, g, 0, 0), pipeline_mode=pl.Buffered(1))
    n_tiles = LANES * SLC_BLOCK // tk
    used = used.reshape(B, G, nq // qb, qb, n_tiles, tk // SLC_BLOCK).max(axis=(3, 5))
    used = (used > 0).astype(jnp.int32).reshape(-1)
    return pl.pallas_call(
        functools.partial(_nsa_attend_kernel, tk=tk, n_tiles=n_tiles),
        grid_spec=pltpu.PrefetchScalarGridSpec(
            num_scalar_prefetch=1,
            grid=(B, G, nq // qb),
            in_specs=[pl.BlockSpec((1, 1, qb, NSA_HPG * Q_BLOCK, 2 * LANES), lambda b, g, i, u: (b, g, i, 0, 0)),
                      kv(2 * LANES), kv(2 * LANES), kv(2 * LANES), kv(2 * LANES),
                      pl.BlockSpec((1, tq, gw), lambda b, g, i, u: (b, i, g)),
                      pl.BlockSpec((1, 1, tq, LANES), lambda b, g, i, u: (b, g, i, 0))],
            out_specs=pl.BlockSpec((1, tq, gw), lambda b, g, i, u: (b, i, g)),
            scratch_shapes=[pltpu.VMEM((rows, tk), F32), pltpu.VMEM((rows, tk), F32),
                            pltpu.VMEM((rows, tk), BF16), pltpu.VMEM((rows, tk), BF16),
                            pltpu.VMEM((rows, LANES), F32), pltpu.VMEM((rows, LANES), F32),
                            pltpu.VMEM((rows, 2 * LANES), F32), pltpu.VMEM((rows, 2 * LANES), F32),
                            pltpu.SMEM((n_tiles + 1,), jnp.int32)]),
        out_shape=jax.ShapeDtypeStruct((B, S, G * gw), BF16),
        compiler_params=_params("parallel", "parallel", "arbitrary"),
        name="nsa_attend",
    )(used, qaug, ks, vs, kw, vw, oc, gs)


def _mla_kernel(q_ref, k_ref, v_ref, o_ref, s_a, s_b, p_a, p_b, m_scr, acc_scr, *, tk):
    i = pl.program_id(2)
    q = q_ref[0]
    m_scr[...] = jnp.full(m_scr.shape, NEG, F32)
    acc_scr[...] = jnp.zeros(acc_scr.shape, F32)
    tail = [(0, lambda r0: _causal_bias(r0, 0, tk) if r0 < tk else None),
            (tk, lambda r0: _causal_bias(r0 - tk, 0, tk))]
    _causal_flash(q, k_ref.at[0], v_ref.at[0], 2 * i, tk, (s_a, p_a), (s_b, p_b), m_scr, acc_scr,
                  tail=tail, n_full_even=True)
    acc = acc_scr[...]
    o_ref[0] = (acc / acc[:, MLA_V:MLA_V + 1]).astype(BF16)


def _mla_attend(qcat, kcat, vcat, *, tk=512):
    B, S, _ = qcat.shape
    tk = min(tk, S // 2)
    t = 2 * tk
    H = MLA_HEADS
    kv = pl.BlockSpec((1, S, LANES), lambda b, h, i: (b, 0, h))
    return pl.pallas_call(
        functools.partial(_mla_kernel, tk=tk),
        grid=(B, H, S // t),
        in_specs=[pl.BlockSpec((1, t, LANES), lambda b, h, i: (b, i, h)), kv, kv],
        out_specs=pl.BlockSpec((1, t, LANES), lambda b, h, i: (b, i, h)),
        out_shape=jax.ShapeDtypeStruct((B, S, H * LANES), BF16),
        scratch_shapes=[pltpu.VMEM((t, tk), F32), pltpu.VMEM((t, tk), F32),
                        pltpu.VMEM((t, tk), BF16), pltpu.VMEM((t, tk), BF16),
                        pltpu.VMEM((t, LANES), F32), pltpu.VMEM((t, LANES), F32)],
        compiler_params=_params("parallel", "parallel", "arbitrary"),
        name="mla_attend",
    )(qcat, kcat, vcat)


def _merge_kernel(x_ref, g_ref, gm_ref, oa_ref, ob_ref, wa_ref, wb_ref, wo_ref, o_ref):
    D = x_ref.shape[2]
    gm = gm_ref[0]
    y = gm[:, :D] * _dot(oa_ref[0], wa_ref[...]) + gm[:, D:] * _dot(ob_ref[0], wb_ref[...])
    o_ref[0] = x_ref[0] + g_ref[0] * _dot(y.astype(BF16), wo_ref[...])


def _merge(x, g, gm, oa, ob, wa, wb, wo, *, tm=512):
    B, S, D = x.shape
    row = lambda w: pl.BlockSpec((1, tm, w), lambda b, i: (b, i, 0))
    res = lambda a: _resident(a.shape, lambda b, i: (0, 0))
    return pl.pallas_call(
        _merge_kernel,
        grid=(B, S // tm),
        in_specs=[row(D), pl.BlockSpec((1, 1, D), lambda b, i: (b, 0, 0)), row(2 * D),
                  row(oa.shape[2]), row(ob.shape[2]), res(wa), res(wb), res(wo)],
        out_specs=row(D),
        out_shape=jax.ShapeDtypeStruct((B, S, D), F32),
        compiler_params=_params("parallel", "parallel"),
        name="merge",
    )(x, g, gm, oa, ob, wa, wb, wo)


def _pad_heads(w, n_heads, width):
    k = w.shape[0]
    w = w.reshape(k, n_heads, width)
    return jnp.pad(w, ((0, 0), (0, 0), (0, LANES - width))).reshape(k, n_heads * LANES)


def _rot_cols(w):
    half = w.shape[-1] // 2
    return jnp.concatenate([-w[..., half:], w[..., :half]], axis=-1)


def _prep_inproj_weights(w_in, w_uq, w_uk, w_uv):
    D = w_in.shape[0]
    o = 0
    seg = {}
    for name, width in (("qa", 512), ("kc", 128), ("vc", 128), ("ks", 128), ("vs", 128), ("kw", 128),
                        ("vw", 128), ("gnsa", 3 * NSA_HEADS), ("cq", MLA_Q_RANK), ("ckv", MLA_KV_RANK),
                        ("kr", MLA_ROPE), ("gm", 2048)):
        seg[name] = w_in[:, o:o + width]
        o += width
    per_g = 3 * NSA_HPG
    gn = jnp.concatenate(
        [jnp.pad(seg["gnsa"][:, per_g * g:per_g * (g + 1)], ((0, 0), (0, LANES - per_g)))
         for g in range(NSA_GROUPS)], axis=1)
    z64 = jnp.zeros((D, MLA_NOPE), F32)
    z32 = jnp.zeros((D, LANES - MLA_NOPE - MLA_ROPE), F32)
    kr = jnp.concatenate([z64, seg["kr"], z32, z64, _rot_cols(seg["kr"]), z32], axis=1)
    w_all = jnp.concatenate([seg["qa"], seg["kc"], seg["vc"], seg["ks"], seg["vs"], seg["kw"], seg["vw"],
                             gn, seg["cq"], seg["ckv"], kr, seg["gm"]], axis=1).astype(BF16)
    assert w_all.shape[1] == _SEG_TOTAL
    hd = MLA_NOPE + MLA_ROPE
    uq = w_uq.reshape(-1, MLA_HEADS, hd)
    uq_rot = jnp.concatenate([jnp.zeros_like(uq[..., :MLA_NOPE]), _rot_cols(uq[..., MLA_NOPE:])], axis=-1)
    wqa = _pad_heads(w_uq, MLA_HEADS, hd).astype(BF16)
    wqb = _pad_heads(uq_rot.reshape(-1, MLA_HEADS * hd), MLA_HEADS, hd).astype(BF16)
    wuk = _pad_heads(w_uk, MLA_HEADS, MLA_NOPE).astype(BF16)
    wuv = _pad_heads(w_uv, MLA_HEADS, MLA_V).astype(BF16)
    return w_all, wqa, wqb, wuk, wuv


def _prep_compress_weights(k_w1, k_w2, k_pe, v_w1, v_w2, v_pe):
    d = NSA_D
    half = CMP_STRIDE

    def blockdiag(mats):
        n = len(mats)
        z = jnp.zeros_like(mats[0])
        rows = [jnp.concatenate([mats[r] if c == r else z for c in range(n)], axis=-1) for r in range(n)]
        return jnp.concatenate(rows, axis=-2)

    def first(w1, lo):
        return w1.reshape(CMP_BLOCK, d, d)[lo:lo + half]

    def w1_half(lo):
        k, v = first(k_w1, lo), first(v_w1, lo)
        return blockdiag([k, k, v, v]).reshape(half * 4 * d, 4 * d).astype(BF16)

    def pe_half(lo):
        k, v = k_pe[lo:lo + half], v_pe[lo:lo + half]
        return jnp.concatenate([k, k, v, v], axis=-1).reshape(1, half * 4 * d)

    w2 = blockdiag([k_w2, k_w2, v_w2, v_w2]).astype(BF16)
    return pe_half(0), pe_half(half), w1_half(0), w1_half(half), w2


def _rope_tables(S):
    half = MLA_ROPE // 2
    inv_freq = ROPE_THETA ** (-jnp.arange(half, dtype=F32) / half)
    ang = jnp.arange(S, dtype=F32)[:, None] * inv_freq[None, :]
    cos, sin = jnp.cos(ang), jnp.sin(ang)
    ones = jnp.ones((S, MLA_NOPE), F32)
    z_n = jnp.zeros((S, MLA_NOPE), F32)
    z_p = jnp.zeros((S, LANES - MLA_NOPE - MLA_ROPE), F32)
    ctab = jnp.concatenate([ones, cos, cos, z_p], axis=1)
    stab = jnp.concatenate([z_n, sin, sin, z_p], axis=1)
    return ctab, stab


def kernel(x, c, w_ada, b_ada, ffn1_gate, ffn1_up, ffn1_down, ffn2_gate, ffn2_up, ffn2_down, w_in, cmpk_w1, cmpk_w2, cmpk_pe, cmpv_w1, cmpv_w2, cmpv_pe, mla_q_norm, mla_w_uq, mla_kv_norm, mla_w_uk, mla_w_uv, w_branch_a, w_branch_b, w_out, final_norm):
    B, S, D = x.shape
    L = w_ada.shape[0]
    assert NSA_GROUPS == 2 and S // SLC_BLOCK <= LANES and S // (2 * CMP_STRIDE) <= 256 and S % 512 == 0
    n_top = min(N_SELECT, S // SLC_BLOCK)
    ctab, stab = _rope_tables(S)
    mod = _adaln(c, w_ada, b_ada).reshape(L, B, N_MOD, 1, D)
    bf = lambda w: w.astype(BF16)
    ffn1 = [_to_bf16(w) for w in (ffn1_gate, ffn1_up, ffn1_down)]
    ffn2 = [_to_bf16(w) for w in (ffn2_gate, ffn2_up, ffn2_down)]
    for l in range(L):
        sh1, sc1, g1, sh2, sc2, g2, sh3, sc3, g3 = (mod[l, :, k] for k in range(N_MOD))
        x = _ffn(x, sh1, sc1, g1, *ffn1, l, final_norm, final=False)

        w_all, wqa, wqb, wuk, wuv = _prep_inproj_weights(w_in[l], mla_w_uq[l], mla_w_uk[l], mla_w_uv[l])
        (qa, kcvc, ks, vs, kw, vw, gs, qcat, kcat, vcat, gm) = _inproj(
            x, sh2, sc2, w_all, wqa, wqb, wuk, wuv,
            mla_q_norm[l].reshape(1, -1), mla_kv_norm[l].reshape(1, -1), ctab, stab)
        kc, vct = _compress(kcvc, *_prep_compress_weights(
            cmpk_w1[l], cmpk_w2[l], cmpk_pe[l], cmpv_w1[l], cmpv_w2[l], cmpv_pe[l]))
        oc, qaug, used = _nsa_select(qa, kc, vct, n_top=n_top)
        oa = _nsa_attend(qaug, used, ks, vs, kw, vw, oc, gs)
        ob = _mla_attend(qcat, kcat, vcat)
        wb_pad = jnp.pad(w_branch_b[l].reshape(MLA_HEADS, MLA_V, D),
                         ((0, 0), (0, LANES - MLA_V), (0, 0))).reshape(MLA_HEADS * LANES, D)
        x = _merge(x, g2, gm, oa, ob, bf(w_branch_a[l]), bf(wb_pad), bf(w_out[l]))

        x = _ffn(x, sh3, sc3, g3, *ffn2, l, final_norm, final=(l == L - 1))
    return x
```

```python
import functools
import math

import ml_dtypes
import numpy as np
import jax
import jax.numpy as jnp
from jax import lax
from jax.experimental import pallas as pl
from jax.experimental.pallas import tpu as pltpu

F32 = jnp.float32
BF16 = jnp.bfloat16

NSA_HEADS = 8
NSA_GROUPS = 2
NSA_HPG = NSA_HEADS // NSA_GROUPS
NSA_D = 64
CMP_BLOCK = 32
CMP_STRIDE = 16
SLC_BLOCK = 64
N_SELECT = 16
WINDOW = 512
MLA_HEADS = 8
MLA_NOPE = 64
MLA_ROPE = 32
MLA_V = 64
MLA_Q_RANK = 384
MLA_KV_RANK = 256
ROPE_THETA = 10000.0
N_MOD = 9
Q_BLOCK = 128
NORM_EPS = 1e-6

LANES = 128
MXU_WIDTH = 256
SLC_SHIFT = SLC_BLOCK.bit_length() - 1
assert 1 << SLC_SHIFT == SLC_BLOCK
LOG2E = math.log2(math.e)
SLOPES = [2.0 ** (-8.0 * (k + 1) / NSA_HEADS) for k in range(NSA_HEADS)]
N_PIECES = 3


def _bf16_pieces(v):
    out = []
    for _ in range(N_PIECES):
        p = float(np.float32(v).astype(ml_dtypes.bfloat16))
        out.append(p)
        v -= p
    return out


SLOPE_PIECES = [_bf16_pieces(v * LOG2E) for v in SLOPES]
NEG = -1e30
VMEM_LIMIT = 56 * 1024 * 1024


def _rms(xf):
    return xf * lax.rsqrt(jnp.mean(xf * xf, -1, keepdims=True) + NORM_EPS)


def _sigmoid(x):
    return 1.0 / (1.0 + jnp.exp(-x))


def _dot(a, b):
    return jnp.dot(a, b, preferred_element_type=F32)


def _dot_nt(a, b, precision=None):
    return lax.dot_general(a, b, (((1,), (1,)), ((), ())), precision=precision,
                           preferred_element_type=F32)


def _params(*sem):
    return pltpu.CompilerParams(dimension_semantics=sem, vmem_limit_bytes=VMEM_LIMIT)


def _resident(shape, index_map):
    return pl.BlockSpec(shape, index_map, pipeline_mode=pl.Buffered(1))


def _adaln_kernel(c_ref, w_ref, b_ref, o_ref):
    c = c_ref[...]
    a = c * _sigmoid(c)
    o_ref[0] = jnp.dot(a, w_ref[0], precision=lax.Precision.HIGHEST,
                       preferred_element_type=F32) + b_ref[0]


def _adaln(c, w_ada, b_ada):
    L, D, N = w_ada.shape
    B = c.shape[0]
    rows = 8
    c_pad = jnp.zeros((rows, D), F32).at[:B].set(c)
    tn = N // 8
    out = pl.pallas_call(
        _adaln_kernel,
        grid=(L, N // tn),
        in_specs=[pl.BlockSpec((rows, D), lambda l, j: (0, 0)),
                  pl.BlockSpec((1, D, tn), lambda l, j: (l, 0, j)),
                  pl.BlockSpec((1, 1, tn), lambda l, j: (l, 0, j))],
        out_specs=pl.BlockSpec((1, rows, tn), lambda l, j: (l, 0, j)),
        out_shape=jax.ShapeDtypeStruct((L, rows, N), F32),
        compiler_params=_params("parallel", "parallel"),
        name="adaln_mod",
    )(c_pad, w_ada, b_ada.reshape(L, 1, N))
    return out[:, :B]


def _ffn_kernel(x_ref, sh_ref, sc_ref, g_ref, wg_ref, wu_ref, wd_ref, fn_ref, o_ref, *, splits, final):
    x = x_ref[0]
    u = (_rms(x) * (1.0 + sc_ref[0]) + sh_ref[0]).astype(BF16)
    acc = jnp.zeros(x.shape, F32)
    for f0, f1 in zip(splits[:-1], splits[1:]):
        hg = _dot(u, wg_ref[:, f0:f1])
        hu = _dot(u, wu_ref[:, f0:f1])
        a = (hg * _sigmoid(hg) * hu).astype(BF16)
        acc = acc + _dot(a, wd_ref[f0:f1, :])
    y = x + (0.5 * g_ref[0]) * acc
    if final:
        y = _rms(y) * fn_ref[...]
    o_ref[0] = y


def _cast_kernel(x_ref, o_ref):
    o_ref[...] = x_ref[...].astype(o_ref.dtype)


def _to_bf16(w, *, tr=256):
    L, R, C = w.shape
    spec = pl.BlockSpec((1, tr, C), lambda l, i: (l, i, 0))
    return pl.pallas_call(
        _cast_kernel, grid=(L, R // tr), in_specs=[spec], out_specs=spec,
        out_shape=jax.ShapeDtypeStruct(w.shape, BF16),
        compiler_params=_params("parallel", "parallel"), name="to_bf16",
    )(w)


def _ffn(x, sh, sc, g, wg, wu, wd, layer, final_norm, *, final, tm=512):
    B, S, D = x.shape
    n_ff = wg.shape[2]
    cut = (n_ff // 2 + MXU_WIDTH - 1) // MXU_WIDTH * MXU_WIDTH
    splits = (0, cut, n_ff) if 0 < cut < n_ff else (0, n_ff)
    vec = pl.BlockSpec((1, 1, D), lambda b, i: (b, 0, 0))
    return pl.pallas_call(
        functools.partial(_ffn_kernel, splits=splits, final=final),
        grid=(B, S // tm),
        in_specs=[pl.BlockSpec((1, tm, D), lambda b, i: (b, i, 0)), vec, vec, vec,
                  _resident((None, D, n_ff), lambda b, i: (layer, 0, 0)),
                  _resident((None, D, n_ff), lambda b, i: (layer, 0, 0)),
                  _resident((None, n_ff, D), lambda b, i: (layer, 0, 0)),
                  pl.BlockSpec((1, D), lambda b, i: (0, 0))],
        out_specs=pl.BlockSpec((1, tm, D), lambda b, i: (b, i, 0)),
        out_shape=jax.ShapeDtypeStruct((B, S, D), F32),
        compiler_params=_params("parallel", "parallel"),
        name="ffn_final" if final else "ffn",
    )(x, sh, sc, g, wg, wu, wd, final_norm.reshape(1, D))


_SEG = {}
_o = 0
for _name, _w in (("qa", 512), ("kcvc", 256), ("kv4", 512), ("gnsa", 256), ("cq", MLA_Q_RANK),
                  ("ckv", MLA_KV_RANK), ("kr", 256), ("gm", 2048)):
    _SEG[_name] = (_o, _o + _w)
    _o += _w
_SEG_TOTAL = _o


def _inproj_kernel(x_ref, sh_ref, sc_ref, w_ref, wqa_ref, wqb_ref, wuk_ref, wuv_ref, qn_ref, kvn_ref,
                   ctab_ref, stab_ref,
                   qa_o, kcvc_o, ks_o, vs_o, kw_o, vw_o, gs_o, qcat_o, kcat_o, vcat_o, gm_o, *, tm):
    i = pl.program_id(1)
    x = x_ref[0]
    u = (_rms(x) * (1.0 + sc_ref[0]) + sh_ref[0]).astype(BF16)

    def proj(name):
        a, b = _SEG[name]
        return _dot(u, w_ref[:, a:b])

    qa_o[0] = proj("qa").astype(BF16)
    kcvc = proj("kcvc")
    kcvc_o[0, 0] = kcvc[:, :LANES]
    kcvc_o[0, 1] = kcvc[:, LANES:]

    kv = proj("kv4")
    lane = lax.broadcasted_iota(jnp.int32, (tm, LANES), 1)
    pos = i * tm + lax.broadcasted_iota(jnp.int32, (tm, LANES), 0)
    blk = jnp.right_shift(pos, SLC_SHIFT)
    rem = pos - blk * SLC_BLOCK
    onehot = jnp.where(lane == blk, 1.0, 0.0).astype(BF16)
    zeros = jnp.zeros((tm, LANES), BF16)
    kextra = jnp.where(lane < NSA_D + N_PIECES, rem.astype(F32),
                       jnp.where(lane < NSA_D + 2 * N_PIECES, blk.astype(F32), 0.0))
    for g in range(NSA_GROUPS):
        for src, k_o, v_o, first in ((0, ks_o, vs_o, onehot), (2, kw_o, vw_o, zeros)):
            kk = kv[:, LANES * src:LANES * (src + 1)]
            vv = kv[:, LANES * (src + 1):LANES * (src + 2)]
            vr = pltpu.roll(vv, NSA_D, 1)
            v_lo, v_hi = (vv, vr) if g == 0 else (vr, vv)
            if g == 1:
                kk = pltpu.roll(kk, NSA_D, 1)
            k_o[0, g, :, 0:LANES] = first
            k_o[0, g, :, LANES:2 * LANES] = jnp.where(lane < NSA_D, kk, kextra).astype(BF16)
            v_o[0, g, :, 0:LANES] = jnp.where(lane < NSA_D, v_lo, 1.0).astype(BF16)
            v_o[0, g, :, LANES:2 * LANES] = jnp.where(lane < NSA_D, 1.0, v_hi).astype(BF16)

    gs = _sigmoid(proj("gnsa"))
    for g in range(NSA_GROUPS):
        gs_o[0, g] = gs[:, LANES * g:LANES * (g + 1)]

    ctab = ctab_ref[...]
    stab = stab_ref[...]
    scale = (MLA_NOPE + MLA_ROPE) ** -0.5 * LOG2E
    cqn =(_rms(proj("cq")) * qn_ref[...]).astype(BF16)
    q_a = _dot(cqn, wqa_ref[...])
    q_b = _dot(cqn, wqb_ref[...])
    ckvn = (_rms(proj("ckv")) * kvn_ref[...]).astype(BF16)
    kn = _dot(ckvn, wuk_ref[...])
    vv = _dot(ckvn, wuv_ref[...])
    kr = proj("kr")
    krope = kr[:, :LANES] * ctab + kr[:, LANES:] * stab
    for h in range(MLA_HEADS):
        sl = slice(LANES * h, LANES * (h + 1))
        qcat_o[0, :, sl] = ((q_a[:, sl] * ctab + q_b[:, sl] * stab) * scale).astype(BF16)
        kcat_o[0, :, sl] = (kn[:, sl] + krope).astype(BF16)
        vcat_o[0, :, sl] = jnp.where(lane == MLA_V, 1.0, vv[:, sl]).astype(BF16)

    gm_o[0] = _sigmoid(proj("gm"))


def _inproj(x, sh, sc, w_all, wqa, wqb, wuk, wuv, qn, kvn, ctab, stab, *, tm=512):
    B, S, D = x.shape
    G = NSA_GROUPS
    grid = (B, S // tm)
    vec = pl.BlockSpec((1, 1, D), lambda b, i: (b, 0, 0))

    def res(a):
        return _resident(a.shape, lambda b, i: (0,) * a.ndim)

    row = lambda w: pl.BlockSpec((1, tm, w), lambda b, i: (b, i, 0))
    grp = lambda w: pl.BlockSpec((1, G, tm, w), lambda b, i: (b, 0, i, 0))
    tab = pl.BlockSpec((tm, LANES), lambda b, i: (i, 0))
    out_shape = (
        jax.ShapeDtypeStruct((B, S, 512), BF16),
        jax.ShapeDtypeStruct((B, 2, S, LANES), F32),
        jax.ShapeDtypeStruct((B, G, S, 256), BF16),
        jax.ShapeDtypeStruct((B, G, S, 256), BF16),
        jax.ShapeDtypeStruct((B, G, S, 256), BF16),
        jax.ShapeDtypeStruct((B, G, S, 256), BF16),
        jax.ShapeDtypeStruct((B, G, S, 128), F32),
        jax.ShapeDtypeStruct((B, S, MLA_HEADS * LANES), BF16),
        jax.ShapeDtypeStruct((B, S, MLA_HEADS * LANES), BF16),
        jax.ShapeDtypeStruct((B, S, MLA_HEADS * LANES), BF16),
        jax.ShapeDtypeStruct((B, S, 2048), F32),
    )
    out_specs = (row(512), grp(LANES), grp(256), grp(256), grp(256), grp(256), grp(128),
                 row(MLA_HEADS * LANES), row(MLA_HEADS * LANES), row(MLA_HEADS * LANES), row(2048))
    return pl.pallas_call(
        functools.partial(_inproj_kernel, tm=tm),
        grid=grid,
        in_specs=[pl.BlockSpec((1, tm, D), lambda b, i: (b, i, 0)), vec, vec,
                  res(w_all), res(wqa), res(wqb), res(wuk), res(wuv), res(qn), res(kvn), tab, tab],
        out_specs=out_specs,
        out_shape=out_shape,
        compiler_params=_params("parallel", "parallel"),
        name="in_proj",
    )(x, sh, sc, w_all, wqa, wqb, wuk, wuv, qn, kvn, ctab, stab)


def _compress_kernel(a_ref, pea_ref, peb_ref, wa_ref, wb_ref, w2_ref, kcat_o, vct_o):
    planes, s_len, w = a_ref.shape[1:]
    n = s_len // CMP_STRIDE
    pa = jnp.zeros((n, planes * w), F32)
    pb = jnp.zeros((n, planes * w), F32)
    for t in range(CMP_STRIDE):
        a = jnp.concatenate([a_ref[0, j, pl.ds(t, n, stride=CMP_STRIDE), :] for j in range(planes)], axis=1)
        ws = slice(planes * w * t, planes * w * (t + 1))
        pa = pa + _dot((a + pea_ref[:, ws]).astype(BF16), wa_ref[ws, :])
        pb = pb + _dot((a + peb_ref[:, ws]).astype(BF16), wb_ref[ws, :])
    h = pa + pltpu.roll(pb, n - 1, 0)
    h = 0.5 * h * (1.0 + jnp.tanh(0.7978845608028654 * (h + 0.044715 * (h * h * h))))
    out = _dot(h.astype(BF16), w2_ref[...])
    out_t = out.T
    lane = lax.broadcasted_iota(jnp.int32, (n, LANES), 1)
    row = lax.broadcasted_iota(jnp.int32, (n, LANES), 0)
    pair = jnp.right_shift(row, 1)
    within = (row - 2 * pair) * CMP_STRIDE + (CMP_BLOCK - 1)
    kextra = jnp.where(lane < NSA_D + N_PIECES, pair.astype(F32),
                       jnp.where(lane < NSA_D + 2 * N_PIECES, within.astype(F32), 0.0))
    kk = out[:, 0:LANES] * (NSA_D ** -0.5 * LOG2E)
    for g in range(NSA_GROUPS):
        x = kk if g == 0 else pltpu.roll(kk, NSA_D, 1)
        lo = x - x.astype(BF16).astype(F32)
        kcat_o[0, g, :, 0:LANES] = jnp.where(lane < NSA_D, x, kextra).astype(BF16)
        kcat_o[0, g, :, LANES:2 * LANES] = jnp.where(lane < NSA_D, lo, 0.0).astype(BF16)
        vct_o[0, g] = out_t[2 * NSA_D + NSA_D * g:2 * NSA_D + NSA_D * (g + 1), :].astype(BF16)


def _compress(kcvc, pea, peb, wa, wb, w2):
    B, P, S, W = kcvc.shape
    n = S // CMP_STRIDE
    G = NSA_GROUPS
    full = lambda arr: _resident(arr.shape, lambda b: (0,) * arr.ndim)
    return pl.pallas_call(
        _compress_kernel,
        grid=(B,),
        in_specs=[pl.BlockSpec((1, P, S, W), lambda b: (b, 0, 0, 0)),
                  full(pea), full(peb), full(wa), full(wb), full(w2)],
        out_specs=(pl.BlockSpec((1, G, n, 2 * LANES), lambda b: (b, 0, 0, 0)),
                   pl.BlockSpec((1, G, NSA_D, n), lambda b: (b, 0, 0, 0))),
        out_shape=(jax.ShapeDtypeStruct((B, G, n, 2 * LANES), BF16),
                   jax.ShapeDtypeStruct((B, G, NSA_D, n), BF16)),
        compiler_params=_params("parallel"),
        name="compress",
    )(kcvc, pea, peb, wa, wb, w2)


def _nsa_select_kernel(q_ref, kcat_ref, vct_ref, oc_o, qaug_o, used_o, qc_scr, s_scr, p_scr, *, n_top):
    g = pl.program_id(1)
    i = pl.program_id(2)
    tq = Q_BLOCK
    qb = q_ref.shape[1] // tq
    tqs = qb * tq
    ncp = kcat_ref.shape[2]
    q0 = i * tqs
    scale = NSA_D ** -0.5

    lane = lax.broadcasted_iota(jnp.int32, (tq, LANES), 1)
    for c in range(qb):
        qf = q_ref[0, tq * c:tq * (c + 1), :].astype(F32)
        for h in range(NSA_HPG):
            rs = slice(tq * h, tq * (h + 1))
            x = qf[:, LANES * (h // 2):LANES * (h // 2 + 1)]
            if h % 2 == 1:
                x = pltpu.roll(x, NSA_D, 1)
            extra = jnp.zeros((tq, LANES), F32)
            extra_c = jnp.zeros((tq, LANES), F32)
            for k in range(N_PIECES):
                piece = jnp.where(g == 0, SLOPE_PIECES[h][k], SLOPE_PIECES[NSA_HPG + h][k])
                extra = jnp.where(lane == NSA_D + k, piece,
                                  jnp.where(lane == NSA_D + N_PIECES + k, piece * SLC_BLOCK, extra))
                extra_c = jnp.where(lane == NSA_D + k, piece * (2 * CMP_STRIDE),
                                    jnp.where(lane == NSA_D + N_PIECES + k, piece, extra_c))
            qaug_o[0, 0, c, rs, LANES:2 * LANES] = jnp.where(lane < NSA_D, x * (scale * LOG2E), extra).astype(BF16)
            rc = slice(NSA_HPG * tq * c + tq * h, NSA_HPG * tq * c + tq * (h + 1))
            qc_scr[rc, 0:LANES] = jnp.where(lane < NSA_D, x, extra_c).astype(BF16)
            qc_scr[rc, LANES:2 * LANES] = jnp.where(lane < NSA_D, x, 0.0).astype(BF16)

    s_scr[...] = _dot_nt(kcat_ref[0, 0], qc_scr[...])
    cmp_end = lax.broadcasted_iota(jnp.int32, (ncp, tq), 0) * CMP_STRIDE + (CMP_BLOCK - 1)
    psums, inv = [], []
    for c in range(qb):
        mask_c = cmp_end <= q0 + tq * c + lax.broadcasted_iota(jnp.int32, (ncp, tq), 1)
        psum = jnp.zeros((ncp, tq), F32)
        for h in range(NSA_HPG):
            cs = slice(tq * (NSA_HPG * c + h), tq * (NSA_HPG * c + h + 1))
            s = jnp.where(mask_c, s_scr[:, cs], -jnp.inf)
            m = jnp.max(s, axis=0, keepdims=True)
            m = jnp.where(m == -jnp.inf, 0.0, m)
            p = jnp.exp2(s - m)
            r = 1.0 / jnp.maximum(jnp.sum(p, axis=0, keepdims=True), 1.0)
            p_scr[:, cs] = p.astype(BF16)
            psum = psum + p * r
            inv.append(r)
        psums.append(psum)
    oc_t = _dot(vct_ref[0, 0], p_scr[...])
    for c in range(qb):
        oc_o[0, tq * c:tq * (c + 1), :] = jnp.concatenate(
            [oc_t[:, tq * (NSA_HPG * c + h):tq * (NSA_HPG * c + h + 1)] * inv[NSA_HPG * c + h]
             for h in range(NSA_HPG)], axis=0).T
    psum = jnp.concatenate(psums, axis=1)
    tq = tqs

    ratio = SLC_BLOCK // CMP_STRIDE
    r = CMP_BLOCK // CMP_STRIDE
    bb = lax.broadcasted_iota(jnp.int32, (LANES, ncp), 0)
    mm = lax.broadcasted_iota(jnp.int32, (LANES, ncp), 1)
    a_t = jnp.where((mm >= ratio * bb - (r - 1)) & (mm <= ratio * bb + ratio - 1), 1.0, 0.0).astype(BF16)
    p_hi = psum.astype(BF16)
    p_lo = (psum - p_hi.astype(F32)).astype(BF16)
    imp = _dot(a_t, p_hi) + _dot(a_t, p_lo)

    blk = lax.broadcasted_iota(jnp.int32, (LANES, tq), 0)
    t_s = q0 + lax.broadcasted_iota(jnp.int32, (LANES, tq), 1)
    cur = jnp.right_shift(t_s, SLC_SHIFT)
    forced = (blk == 0) | (blk == cur) | (blk == cur - 1)
    valid = blk * SLC_BLOCK <= t_s
    cur_q = jnp.right_shift(q0 + lax.broadcasted_iota(jnp.int32, (1, tq), 1), SLC_SHIFT)
    n_forced = 1 + jnp.where(cur_q >= 1, 1, 0) + jnp.where(cur_q >= 2, 1, 0)
    left = (n_top - n_forced).astype(F32)
    cand = jnp.where(forced, -jnp.inf, jnp.where(valid, imp, -jnp.inf))
    blkf = blk.astype(F32)

    def pick_one(r, carry):
        cand, sel = carry
        m = jnp.max(cand, axis=0, keepdims=True)
        idx = jnp.min(jnp.where(cand == m, blkf, float(LANES)), axis=0, keepdims=True)
        pick = blkf == jnp.where(left > r.astype(F32), idx, -1.0)
        return jnp.where(pick, -jnp.inf, cand), jnp.where(pick, 1.0, sel)

    n_rounds = jnp.where(q0 >= 2 * SLC_BLOCK, n_top - 3, n_top - 1)
    _, sel = lax.fori_loop(0, jnp.maximum(n_rounds, 0), pick_one, (cand, jnp.where(forced, 1.0, 0.0)))
    pen_t = jnp.where(valid, jnp.where(sel > 0.0, 0.0, NEG), NEG)
    pen = pen_t.T
    for c in range(qb):
        pen_c = pen[Q_BLOCK * c:Q_BLOCK * (c + 1)]
        for h in range(NSA_HPG):
            qaug_o[0, 0, c, Q_BLOCK * h:Q_BLOCK * (h + 1), 0:LANES] = pen_c.astype(BF16)
        used_o[0, 0, c] = jnp.where(jnp.max(pen_c, axis=0, keepdims=True) == 0.0, 1.0, 0.0)


def _nsa_select(qa, kc, vct, *, n_top, qb=2):
    B, S, _ = qa.shape
    G = NSA_GROUPS
    tq = Q_BLOCK
    nq = S // tq
    ncp = kc.shape[2]
    gw = NSA_HPG * NSA_D
    rows = qb * NSA_HPG * tq
    return pl.pallas_call(
        functools.partial(_nsa_select_kernel, n_top=n_top),
        grid=(B, G, nq // qb),
        in_specs=[pl.BlockSpec((1, qb * tq, gw), lambda b, g, i: (b, i, g)),
                  pl.BlockSpec((1, 1, ncp, 2 * LANES), lambda b, g, i: (b, g, 0, 0)),
                  pl.BlockSpec((1, 1, NSA_D, ncp), lambda b, g, i: (b, g, 0, 0))],
        out_specs=(pl.BlockSpec((1, qb * tq, gw), lambda b, g, i: (b, i, g)),
                   pl.BlockSpec((1, 1, qb, NSA_HPG * tq, 2 * LANES), lambda b, g, i: (b, g, i, 0, 0)),
                   pl.BlockSpec((1, 1, qb, 1, LANES), lambda b, g, i: (b, g, i, 0, 0))),
        out_shape=(jax.ShapeDtypeStruct((B, S, G * gw), F32),
                   jax.ShapeDtypeStruct((B, G, nq, NSA_HPG * tq, 2 * LANES), BF16),
                   jax.ShapeDtypeStruct((B, G, nq, 1, LANES), F32)),
        scratch_shapes=[pltpu.VMEM((rows, 2 * LANES), BF16),
                        pltpu.VMEM((ncp, rows), F32), pltpu.VMEM((ncp, rows), BF16)],
        compiler_params=_params("parallel", "parallel", "parallel"),
        name="nsa_select",
    )(qa, kc, vct)


CHUNK = 64


def _scores(q, k, buf, row0=0):
    buf[0][row0:, :k.shape[0]] = _dot_nt(q[row0:], k)


def _softmax_pv(buf, v, m_scr, acc_scr, bias_fn=None, bias_key=None, row0=0):
    s_scr, p_scr = buf
    rows, n = m_scr.shape[0], v.shape[0]
    biases = {}
    for r0 in range(row0, rows, CHUNK):
        rs = slice(r0, r0 + CHUNK)
        s = s_scr[rs, :n]
        if bias_fn is not None:
            key = r0 if bias_key is None else bias_key(r0)
            if key not in biases:
                biases[key] = bias_fn(r0)
            if biases[key] is not None:
                s = s + biases[key]
        m_old = m_scr[rs]
        m_new = jnp.maximum(m_old, jnp.broadcast_to(jnp.max(s, axis=-1, keepdims=True), m_old.shape))
        m_scr[rs] = m_new
        alpha = jnp.exp2(m_old - m_new)
        for c in range(0, acc_scr.shape[1], LANES):
            acc_scr[rs, c:c + LANES] = acc_scr[rs, c:c + LANES] * alpha
        for c in range(0, n, LANES):
            p_scr[rs, c:c + LANES] = jnp.exp2((s[:, c:c + LANES] - m_new).astype(BF16))
    acc_scr[row0:] += _dot(p_scr[row0:, :n], v)


def _causal_flash(chains, n_full, tk, tail, bias_key=None, overlap_fn=None, n_full_even=False,
                  tile_at=lambda p: p):
    off = lambda p: pl.multiple_of(tile_at(p) * tk, tk)
    n_pairs = lax.div(n_full, 2)

    def scores(p, which, row0=0):
        for q, k_fn, _, buf_a, buf_b, _, _ in chains:
            _scores(q, k_fn(off(p)), (buf_a, buf_b)[which], row0=row0)

    def softmax_pv(p, which, bias_fn=None, row0=0):
        for _, _, v_fn, buf_a, buf_b, m_scr, acc_scr in chains:
            _softmax_pv((buf_a, buf_b)[which], v_fn(off(p)), m_scr, acc_scr, bias_fn=bias_fn, bias_key=bias_key,
                        row0=row0)

    def finish(j, steps):
        for k, (row0, bias_fn) in enumerate(steps):
            if k + 1 < len(steps):
                scores(j + k + 1, (k + 1) % 2, row0=steps[k + 1][0])
            softmax_pv(j + k, k % 2, bias_fn=bias_fn, row0=row0)

    scores(0, 0)
    if overlap_fn is not None:
        overlap_fn()

    def pair(jj, carry):
        j = 2 * jj
        scores(j + 1, 1)
        softmax_pv(j, 0)
        scores(j + 2, 0)
        softmax_pv(j + 1, 1)
        return carry

    lax.fori_loop(0, n_pairs, pair, 0)
    j = 2 * n_pairs
    if n_full_even:
        finish(j, tail)
    else:
        pl.when(j < n_full)(lambda: finish(j, [(0, None)] + tail))
        pl.when(j == n_full)(lambda: finish(j, tail))


def _causal_bias(t0, pos0, n):
    t = t0 + lax.broadcasted_iota(jnp.int32, (CHUNK, n), 0)
    pos = pos0 + lax.broadcasted_iota(jnp.int32, (CHUNK, n), 1)
    return jnp.where(pos <= t, 0.0, NEG)


def _nsa_attend_kernel(used_ref, qaug_ref, ks_ref, vs_ref, kw_ref, vw_ref, oc_ref, gs_ref, o_ref,
                       s_a, s_b, p_a, p_b, m_scr, mw_scr, accs_scr, accw_scr, tiles_ref, *, tk, n_tiles):
    i = pl.program_id(2)
    step = (pl.program_id(0) * pl.num_programs(1) + pl.program_id(1)) * pl.num_programs(2) + i
    buf_a, buf_b = (s_a, p_a), (s_b, p_b)
    qb = qaug_ref.shape[2]
    blk_rows = NSA_HPG * Q_BLOCK
    rows = qb * blk_rows
    q = qaug_ref[0, 0].reshape(rows, 2 * LANES)
    q0 = i * (qb * Q_BLOCK)
    row_dt = lambda r0: (r0 // blk_rows) * Q_BLOCK + r0 % Q_BLOCK

    offw = jnp.maximum(q0 - WINDOW, 0)
    w_off = lambda w: pl.multiple_of(offw + w * tk, Q_BLOCK)

    def window_bias(w):
        def bias(r0):
            d = (q0 + row_dt(r0) + lax.broadcasted_iota(jnp.int32, (CHUNK, tk), 0)
                 - (w_off(w) + lax.broadcasted_iota(jnp.int32, (CHUNK, tk), 1)))
            return jnp.where(d >= 0, jnp.where(d < WINDOW, 0.0, NEG), NEG)
        return bias

    def window_softmax(w, buf):
        _softmax_pv(buf, vw_ref[0, 0, pl.ds(w_off(w), tk), :], mw_scr, accw_scr,
                    bias_fn=window_bias(w), bias_key=row_dt)

    mw_scr[...] = jnp.full((rows, LANES), NEG, F32)
    accw_scr[...] = jnp.zeros(accw_scr.shape, F32)
    _scores(q, kw_ref[0, 0, pl.ds(w_off(0), tk), :], buf_a)
    _scores(q, kw_ref[0, 0, pl.ds(w_off(1), tk), :], buf_b)
    window_softmax(0, buf_a)

    m_scr[...] = jnp.full((rows, LANES), NEG, F32)
    accs_scr[...] = jnp.zeros(accs_scr.shape, F32)
    n_full = lax.div(q0, tk)
    n_used = jnp.int32(0)
    for t in range(n_tiles):
        tiles_ref[n_used] = t
        n_used = n_used + jnp.where((used_ref[step * n_tiles + t] > 0) & (t < n_full), 1, 0)
    tile_at = lambda p: jnp.where(p < n_used, tiles_ref[jnp.minimum(p, n_tiles)], n_full + p - n_used)
    chain = (q, lambda o: ks_ref[0, 0, pl.ds(o, tk), :], lambda o: vs_ref[0, 0, pl.ds(o, tk), :],
             buf_a, buf_b, m_scr, accs_scr)
    _causal_flash([chain], n_used, tk,
                  tail=[(0, lambda r0: _causal_bias(q0 + row_dt(r0), n_full * tk, tk))], bias_key=row_dt,
                  overlap_fn=lambda: window_softmax(1, buf_b), tile_at=tile_at)

    gw = NSA_HPG * NSA_D
    gs = gs_ref[0, 0]
    src = lax.broadcasted_iota(jnp.int32, (LANES, 3 * gw), 0)
    dst = lax.broadcasted_iota(jnp.int32, (LANES, 3 * gw), 1)
    branch = jnp.right_shift(dst, gw.bit_length() - 1)
    head = jnp.right_shift(dst & (gw - 1), NSA_D.bit_length() - 1)
    spread = jnp.where(src == 3 * head + branch, 1.0, 0.0).astype(BF16)
    gs_hi = gs.astype(BF16)
    gs_lo = (gs - gs_hi.astype(F32)).astype(BF16)
    g_exp = _dot(gs_hi, spread) + _dot(gs_lo, spread)

    lane = lax.broadcasted_iota(jnp.int32, (Q_BLOCK, LANES), 1)

    def normalised(acc_ref, c, h2):
        r0 = blk_rows * c + Q_BLOCK * 2 * h2
        even = acc_ref[r0:r0 + Q_BLOCK, :]
        odd = acc_ref[r0 + Q_BLOCK:r0 + 2 * Q_BLOCK, :]
        return jnp.where(lane < NSA_D, even[:, :LANES] / even[:, LANES:], odd[:, LANES:] / odd[:, :LANES])

    for c in range(qb):
        qs = slice(Q_BLOCK * c, Q_BLOCK * (c + 1))
        for h2 in range(NSA_HPG // 2):
            ls = slice(LANES * h2, LANES * (h2 + 1))
            gate = lambda j: g_exp[qs, gw * j + LANES * h2:gw * j + LANES * (h2 + 1)]
            o = (gate(0) * oc_ref[0, qs, ls] + gate(1) * normalised(accs_scr, c, h2)
                 + gate(2) * normalised(accw_scr, c, h2))
            o_ref[0, qs, ls] = o.astype(BF16)


def _nsa_attend(qaug, used, ks, vs, kw, vw, oc, gs, *, tk=512, qb=4):
    B, G, nq = qaug.shape[:3]
    S = ks.shape[2]
    tq = qb * Q_BLOCK
    tk = min(tk, S)
    gw = NSA_HPG * NSA_D
    rows = NSA_HPG * tq
    nw = WINDOW + tq
    assert S >= nw and nq % qb == 0 and tk % tq == 0 and nw == 2 * tk
    kv = lambda w: pl.BlockSpec((1, 1, S, w), lambda b, g, i, u: (b, g, 0, 0), pipeline_mode=pl.Buffered(1))
    n_tiles = LANES * SLC_BLOCK // tk
    used = used.reshape(B, G, nq // qb, qb, n_tiles, tk // SLC_BLOCK).max(axis=(3, 5))
    used = (used > 0).astype(jnp.int32).reshape(-1)
    return pl.pallas_call(
        functools.partial(_nsa_attend_kernel, tk=tk, n_tiles=n_tiles),
        grid_spec=pltpu.PrefetchScalarGridSpec(
            num_scalar_prefetch=1,
            grid=(B, G, nq // qb),
            in_specs=[pl.BlockSpec((1, 1, qb, NSA_HPG * Q_BLOCK, 2 * LANES), lambda b, g, i, u: (b, g, i, 0, 0)),
                      kv(2 * LANES), kv(2 * LANES), kv(2 * LANES), kv(2 * LANES),
                      pl.BlockSpec((1, tq, gw), lambda b, g, i, u: (b, i, g)),
                      pl.BlockSpec((1, 1, tq, LANES), lambda b, g, i, u: (b, g, i, 0))],
            out_specs=pl.BlockSpec((1, tq, gw), lambda b, g, i, u: (b, i, g)),
            scratch_shapes=[pltpu.VMEM((rows, tk), F32), pltpu.VMEM((rows, tk), F32),
                            pltpu.VMEM((rows, tk), BF16), pltpu.VMEM((rows, tk), BF16),
                            pltpu.VMEM((rows, LANES), F32), pltpu.VMEM((rows, LANES), F32),
                            pltpu.VMEM((rows, 2 * LANES), F32), pltpu.VMEM((rows, 2 * LANES), F32),
                            pltpu.SMEM((n_tiles + 1,), jnp.int32)]),
        out_shape=jax.ShapeDtypeStruct((B, S, G * gw), BF16),
        compiler_params=_params("parallel", "parallel", "arbitrary"),
        name="nsa_attend",
    )(used, qaug, ks, vs, kw, vw, oc, gs)


def _mla_kernel(q_ref, k_ref, v_ref, o_ref, *scratch, tk, hps):
    i = pl.program_id(2)
    tail = [(0, lambda r0: _causal_bias(r0, 0, tk) if r0 < tk else None),
            (tk, lambda r0: _causal_bias(r0 - tk, 0, tk))]
    chains = []
    for h in range(hps):
        s_a, s_b, p_a, p_b, m_scr, acc_scr = scratch[6 * h:6 * (h + 1)]
        ls = slice(LANES * h, LANES * (h + 1))
        m_scr[...] = jnp.full(m_scr.shape, NEG, F32)
        acc_scr[...] = jnp.zeros(acc_scr.shape, F32)
        chains.append((q_ref[0, :, ls],
                       lambda o, ls=ls: k_ref[0, pl.ds(o, tk), ls], lambda o, ls=ls: v_ref[0, pl.ds(o, tk), ls],
                       (s_a, p_a), (s_b, p_b), m_scr, acc_scr))
    _causal_flash(chains, 2 * i, tk, tail=tail, n_full_even=True)
    for h in range(hps):
        acc = scratch[6 * h + 5][...]
        o_ref[0, :, LANES * h:LANES * (h + 1)] = (acc / acc[:, MLA_V:MLA_V + 1]).astype(BF16)


def _mla_attend(qcat, kcat, vcat, *, tk=512, hps=2):
    B, S, _ = qcat.shape
    tk = min(tk, S // 2)
    t = 2 * tk
    H = MLA_HEADS
    wl = hps * LANES
    kv = pl.BlockSpec((1, S, wl), lambda b, h, i: (b, 0, h))
    per_head = [pltpu.VMEM((t, tk), F32), pltpu.VMEM((t, tk), F32),
                pltpu.VMEM((t, tk), BF16), pltpu.VMEM((t, tk), BF16),
                pltpu.VMEM((t, LANES), F32), pltpu.VMEM((t, LANES), F32)]
    return pl.pallas_call(
        functools.partial(_mla_kernel, tk=tk, hps=hps),
        grid=(B, H // hps, S // t),
        in_specs=[pl.BlockSpec((1, t, wl), lambda b, h, i: (b, i, h)), kv, kv],
        out_specs=pl.BlockSpec((1, t, wl), lambda b, h, i: (b, i, h)),
        out_shape=jax.ShapeDtypeStruct((B, S, H * LANES), BF16),
        scratch_shapes=per_head * hps,
        compiler_params=_params("parallel", "parallel", "arbitrary"),
        name="mla_attend",
    )(qcat, kcat, vcat)


def _merge_kernel(x_ref, g_ref, gm_ref, oa_ref, ob_ref, wa_ref, wb_ref, wo_ref, o_ref):
    D = x_ref.shape[2]
    gm = gm_ref[0]
    y = gm[:, :D] * _dot(oa_ref[0], wa_ref[...]) + gm[:, D:] * _dot(ob_ref[0], wb_ref[...])
    o_ref[0] = x_ref[0] + g_ref[0] * _dot(y.astype(BF16), wo_ref[...])


def _merge(x, g, gm, oa, ob, wa, wb, wo, *, tm=512):
    B, S, D = x.shape
    row = lambda w: pl.BlockSpec((1, tm, w), lambda b, i: (b, i, 0))
    res = lambda a: _resident(a.shape, lambda b, i: (0, 0))
    return pl.pallas_call(
        _merge_kernel,
        grid=(B, S // tm),
        in_specs=[row(D), pl.BlockSpec((1, 1, D), lambda b, i: (b, 0, 0)), row(2 * D),
                  row(oa.shape[2]), row(ob.shape[2]), res(wa), res(wb), res(wo)],
        out_specs=row(D),
        out_shape=jax.ShapeDtypeStruct((B, S, D), F32),
        compiler_params=_params("parallel", "parallel"),
        name="merge",
    )(x, g, gm, oa, ob, wa, wb, wo)


def _pad_heads(w, n_heads, width):
    k = w.shape[0]
    w = w.reshape(k, n_heads, width)
    return jnp.pad(w, ((0, 0), (0, 0), (0, LANES - width))).reshape(k, n_heads * LANES)


def _rot_cols(w):
    half = w.shape[-1] // 2
    return jnp.concatenate([-w[..., half:], w[..., :half]], axis=-1)


def _prep_inproj_weights(w_in, w_uq, w_uk, w_uv):
    D = w_in.shape[0]
    o = 0
    seg = {}
    for name, width in (("qa", 512), ("kc", 128), ("vc", 128), ("ks", 128), ("vs", 128), ("kw", 128),
                        ("vw", 128), ("gnsa", 3 * NSA_HEADS), ("cq", MLA_Q_RANK), ("ckv", MLA_KV_RANK),
                        ("kr", MLA_ROPE), ("gm", 2048)):
        seg[name] = w_in[:, o:o + width]
        o += width
    per_g = 3 * NSA_HPG
    gn = jnp.concatenate(
        [jnp.pad(seg["gnsa"][:, per_g * g:per_g * (g + 1)], ((0, 0), (0, LANES - per_g)))
         for g in range(NSA_GROUPS)], axis=1)
    z64 = jnp.zeros((D, MLA_NOPE), F32)
    z32 = jnp.zeros((D, LANES - MLA_NOPE - MLA_ROPE), F32)
    kr = jnp.concatenate([z64, seg["kr"], z32, z64, _rot_cols(seg["kr"]), z32], axis=1)
    w_all = jnp.concatenate([seg["qa"], seg["kc"], seg["vc"], seg["ks"], seg["vs"], seg["kw"], seg["vw"],
                             gn, seg["cq"], seg["ckv"], kr, seg["gm"]], axis=1).astype(BF16)
    assert w_all.shape[1] == _SEG_TOTAL
    hd = MLA_NOPE + MLA_ROPE
    uq = w_uq.reshape(-1, MLA_HEADS, hd)
    uq_rot = jnp.concatenate([jnp.zeros_like(uq[..., :MLA_NOPE]), _rot_cols(uq[..., MLA_NOPE:])], axis=-1)
    wqa = _pad_heads(w_uq, MLA_HEADS, hd).astype(BF16)
    wqb = _pad_heads(uq_rot.reshape(-1, MLA_HEADS * hd), MLA_HEADS, hd).astype(BF16)
    wuk = _pad_heads(w_uk, MLA_HEADS, MLA_NOPE).astype(BF16)
    wuv = _pad_heads(w_uv, MLA_HEADS, MLA_V).astype(BF16)
    return w_all, wqa, wqb, wuk, wuv


def _prep_compress_weights(k_w1, k_w2, k_pe, v_w1, v_w2, v_pe):
    d = NSA_D
    half = CMP_STRIDE

    def blockdiag(mats):
        n = len(mats)
        z = jnp.zeros_like(mats[0])
        rows = [jnp.concatenate([mats[r] if c == r else z for c in range(n)], axis=-1) for r in range(n)]
        return jnp.concatenate(rows, axis=-2)

    def first(w1, lo):
        return w1.reshape(CMP_BLOCK, d, d)[lo:lo + half]

    def w1_half(lo):
        k, v = first(k_w1, lo), first(v_w1, lo)
        return blockdiag([k, k, v, v]).reshape(half * 4 * d, 4 * d).astype(BF16)

    def pe_half(lo):
        k, v = k_pe[lo:lo + half], v_pe[lo:lo + half]
        return jnp.concatenate([k, k, v, v], axis=-1).reshape(1, half * 4 * d)

    w2 = blockdiag([k_w2, k_w2, v_w2, v_w2]).astype(BF16)
    return pe_half(0), pe_half(half), w1_half(0), w1_half(half), w2


def _rope_tables(S):
    half = MLA_ROPE // 2
    inv_freq = ROPE_THETA ** (-jnp.arange(half, dtype=F32) / half)
    ang = jnp.arange(S, dtype=F32)[:, None] * inv_freq[None, :]
    cos, sin = jnp.cos(ang), jnp.sin(ang)
    ones = jnp.ones((S, MLA_NOPE), F32)
    z_n = jnp.zeros((S, MLA_NOPE), F32)
    z_p = jnp.zeros((S, LANES - MLA_NOPE - MLA_ROPE), F32)
    ctab = jnp.concatenate([ones, cos, cos, z_p], axis=1)
    stab = jnp.concatenate([z_n, sin, sin, z_p], axis=1)
    return ctab, stab


def kernel(x, c, w_ada, b_ada, ffn1_gate, ffn1_up, ffn1_down, ffn2_gate, ffn2_up, ffn2_down, w_in, cmpk_w1, cmpk_w2, cmpk_pe, cmpv_w1, cmpv_w2, cmpv_pe, mla_q_norm, mla_w_uq, mla_kv_norm, mla_w_uk, mla_w_uv, w_branch_a, w_branch_b, w_out, final_norm):
    B, S, D = x.shape
    L = w_ada.shape[0]
    assert NSA_GROUPS == 2 and S // SLC_BLOCK <= LANES and S // (2 * CMP_STRIDE) <= 256 and S % 512 == 0
    n_top = min(N_SELECT, S // SLC_BLOCK)
    ctab, stab = _rope_tables(S)
    mod = _adaln(c, w_ada, b_ada).reshape(L, B, N_MOD, 1, D)
    bf = lambda w: w.astype(BF16)
    ffn1 = [_to_bf16(w) for w in (ffn1_gate, ffn1_up, ffn1_down)]
    ffn2 = [_to_bf16(w) for w in (ffn2_gate, ffn2_up, ffn2_down)]
    for l in range(L):
        sh1, sc1, g1, sh2, sc2, g2, sh3, sc3, g3 = (mod[l, :, k] for k in range(N_MOD))
        x = _ffn(x, sh1, sc1, g1, *ffn1, l, final_norm, final=False)

        w_all, wqa, wqb, wuk, wuv = _prep_inproj_weights(w_in[l], mla_w_uq[l], mla_w_uk[l], mla_w_uv[l])
        (qa, kcvc, ks, vs, kw, vw, gs, qcat, kcat, vcat, gm) = _inproj(
            x, sh2, sc2, w_all, wqa, wqb, wuk, wuv,
            mla_q_norm[l].reshape(1, -1), mla_kv_norm[l].reshape(1, -1), ctab, stab)
        kc, vct = _compress(kcvc, *_prep_compress_weights(
            cmpk_w1[l], cmpk_w2[l], cmpk_pe[l], cmpv_w1[l], cmpv_w2[l], cmpv_pe[l]))
        oc, qaug, used = _nsa_select(qa, kc, vct, n_top=n_top)
        oa = _nsa_attend(qaug, used, ks, vs, kw, vw, oc, gs)
        ob = _mla_attend(qcat, kcat, vcat)
        wb_pad = jnp.pad(w_branch_b[l].reshape(MLA_HEADS, MLA_V, D),
                         ((0, 0), (0, LANES - MLA_V), (0, 0))).reshape(MLA_HEADS * LANES, D)
        x = _merge(x, g2, gm, oa, ob, bf(w_branch_a[l]), bf(wb_pad), bf(w_out[l]))

        x = _ffn(x, sh3, sc3, g3, *ffn2, l, final_norm, final=(l == L - 1))
    return x
```

```python
import functools
import math

import ml_dtypes
import numpy as np
import jax
import jax.numpy as jnp
from jax import lax
from jax.experimental import pallas as pl
from jax.experimental.pallas import tpu as pltpu

F32 = jnp.float32
BF16 = jnp.bfloat16

NSA_HEADS = 8
NSA_GROUPS = 2
NSA_HPG = NSA_HEADS // NSA_GROUPS
NSA_D = 64
CMP_BLOCK = 32
CMP_STRIDE = 16
SLC_BLOCK = 64
N_SELECT = 16
WINDOW = 512
MLA_HEADS = 8
MLA_NOPE = 64
MLA_ROPE = 32
MLA_V = 64
MLA_Q_RANK = 384
MLA_KV_RANK = 256
ROPE_THETA = 10000.0
N_MOD = 9
Q_BLOCK = 128
NORM_EPS = 1e-6

LANES = 128
MXU_WIDTH = 256
SLC_SHIFT = SLC_BLOCK.bit_length() - 1
assert 1 << SLC_SHIFT == SLC_BLOCK
LOG2E = math.log2(math.e)
SLOPES = [2.0 ** (-8.0 * (k + 1) / NSA_HEADS) for k in range(NSA_HEADS)]
N_PIECES = 3


def _bf16_pieces(v):
    out = []
    for _ in range(N_PIECES):
        p = float(np.float32(v).astype(ml_dtypes.bfloat16))
        out.append(p)
        v -= p
    return out


SLOPE_PIECES = [_bf16_pieces(v * LOG2E) for v in SLOPES]
NEG = -1e30
VMEM_LIMIT = 56 * 1024 * 1024


def _rms(xf):
    return xf * lax.rsqrt(jnp.mean(xf * xf, -1, keepdims=True) + NORM_EPS)


def _sigmoid(x):
    return 1.0 / (1.0 + jnp.exp(-x))


def _dot(a, b):
    return jnp.dot(a, b, preferred_element_type=F32)


def _dot_nt(a, b, precision=None):
    return lax.dot_general(a, b, (((1,), (1,)), ((), ())), precision=precision,
                           preferred_element_type=F32)


def _params(*sem):
    return pltpu.CompilerParams(dimension_semantics=sem, vmem_limit_bytes=VMEM_LIMIT)


def _resident(shape, index_map):
    return pl.BlockSpec(shape, index_map, pipeline_mode=pl.Buffered(1))


def _adaln_kernel(c_ref, w_ref, b_ref, o_ref):
    c = c_ref[...]
    a = c * _sigmoid(c)
    o_ref[0] = jnp.dot(a, w_ref[0], precision=lax.Precision.HIGHEST,
                       preferred_element_type=F32) + b_ref[0]


def _adaln(c, w_ada, b_ada):
    L, D, N = w_ada.shape
    B = c.shape[0]
    rows = 8
    c_pad = jnp.zeros((rows, D), F32).at[:B].set(c)
    tn = N // 8
    out = pl.pallas_call(
        _adaln_kernel,
        grid=(L, N // tn),
        in_specs=[pl.BlockSpec((rows, D), lambda l, j: (0, 0)),
                  pl.BlockSpec((1, D, tn), lambda l, j: (l, 0, j)),
                  pl.BlockSpec((1, 1, tn), lambda l, j: (l, 0, j))],
        out_specs=pl.BlockSpec((1, rows, tn), lambda l, j: (l, 0, j)),
        out_shape=jax.ShapeDtypeStruct((L, rows, N), F32),
        compiler_params=_params("parallel", "parallel"),
        name="adaln_mod",
    )(c_pad, w_ada, b_ada.reshape(L, 1, N))
    return out[:, :B]


def _ffn_kernel(x_ref, sh_ref, sc_ref, g_ref, wg_ref, wu_ref, wd_ref, fn_ref, o_ref, *, splits, final):
    x = x_ref[0]
    u = (_rms(x) * (1.0 + sc_ref[0]) + sh_ref[0]).astype(BF16)
    acc = jnp.zeros(x.shape, F32)
    for f0, f1 in zip(splits[:-1], splits[1:]):
        hg = _dot(u, wg_ref[:, f0:f1])
        hu = _dot(u, wu_ref[:, f0:f1])
        a = (hg * _sigmoid(hg) * hu).astype(BF16)
        acc = acc + _dot(a, wd_ref[f0:f1, :])
    y = x + (0.5 * g_ref[0]) * acc
    if final:
        y = _rms(y) * fn_ref[...]
    o_ref[0] = y


def _cast_kernel(x_ref, o_ref):
    o_ref[...] = x_ref[...].astype(o_ref.dtype)


def _to_bf16(w, *, tr=256):
    L, R, C = w.shape
    spec = pl.BlockSpec((1, tr, C), lambda l, i: (l, i, 0))
    return pl.pallas_call(
        _cast_kernel, grid=(L, R // tr), in_specs=[spec], out_specs=spec,
        out_shape=jax.ShapeDtypeStruct(w.shape, BF16),
        compiler_params=_params("parallel", "parallel"), name="to_bf16",
    )(w)


def _ffn(x, sh, sc, g, wg, wu, wd, layer, final_norm, *, final, tm=512):
    B, S, D = x.shape
    n_ff = wg.shape[2]
    cut = (n_ff // 2 + MXU_WIDTH - 1) // MXU_WIDTH * MXU_WIDTH
    splits = (0, cut, n_ff) if 0 < cut < n_ff else (0, n_ff)
    vec = pl.BlockSpec((1, 1, D), lambda b, i: (b, 0, 0))
    return pl.pallas_call(
        functools.partial(_ffn_kernel, splits=splits, final=final),
        grid=(B, S // tm),
        in_specs=[pl.BlockSpec((1, tm, D), lambda b, i: (b, i, 0)), vec, vec, vec,
                  _resident((None, D, n_ff), lambda b, i: (layer, 0, 0)),
                  _resident((None, D, n_ff), lambda b, i: (layer, 0, 0)),
                  _resident((None, n_ff, D), lambda b, i: (layer, 0, 0)),
                  pl.BlockSpec((1, D), lambda b, i: (0, 0))],
        out_specs=pl.BlockSpec((1, tm, D), lambda b, i: (b, i, 0)),
        out_shape=jax.ShapeDtypeStruct((B, S, D), F32),
        compiler_params=_params("parallel", "parallel"),
        name="ffn_final" if final else "ffn",
    )(x, sh, sc, g, wg, wu, wd, final_norm.reshape(1, D))


_SEG = {}
_o = 0
for _name, _w in (("qa", 512), ("kcvc", 256), ("kv4", 512), ("gnsa", 256), ("cq", MLA_Q_RANK),
                  ("ckv", MLA_KV_RANK), ("kr", 256), ("gm", 2048)):
    _SEG[_name] = (_o, _o + _w)
    _o += _w
_SEG_TOTAL = _o


def _inproj_kernel(x_ref, sh_ref, sc_ref, w_ref, wqa_ref, wqb_ref, wuk_ref, wuv_ref, qn_ref, kvn_ref,
                   ctab_ref, stab_ref,
                   qa_o, kcvc_o, ks_o, vs_o, kw_o, vw_o, gs_o, qcat_o, kcat_o, vcat_o, gm_o, *, tm):
    i = pl.program_id(1)
    x = x_ref[0]
    u = (_rms(x) * (1.0 + sc_ref[0]) + sh_ref[0]).astype(BF16)

    def proj(name):
        a, b = _SEG[name]
        return _dot(u, w_ref[:, a:b])

    qa_o[0] = proj("qa").astype(BF16)
    kcvc = proj("kcvc")
    kcvc_o[0, 0] = kcvc[:, :LANES]
    kcvc_o[0, 1] = kcvc[:, LANES:]

    kv = proj("kv4")
    lane = lax.broadcasted_iota(jnp.int32, (tm, LANES), 1)
    pos = i * tm + lax.broadcasted_iota(jnp.int32, (tm, LANES), 0)
    blk = jnp.right_shift(pos, SLC_SHIFT)
    rem = pos - blk * SLC_BLOCK
    onehot = jnp.where(lane == blk, 1.0, 0.0).astype(BF16)
    zeros = jnp.zeros((tm, LANES), BF16)
    kextra = jnp.where(lane < NSA_D + N_PIECES, rem.astype(F32),
                       jnp.where(lane < NSA_D + 2 * N_PIECES, blk.astype(F32), 0.0))
    for g in range(NSA_GROUPS):
        for src, k_o, v_o, first in ((0, ks_o, vs_o, onehot), (2, kw_o, vw_o, zeros)):
            kk = kv[:, LANES * src:LANES * (src + 1)]
            vv = kv[:, LANES * (src + 1):LANES * (src + 2)]
            vr = pltpu.roll(vv, NSA_D, 1)
            v_lo, v_hi = (vv, vr) if g == 0 else (vr, vv)
            if g == 1:
                kk = pltpu.roll(kk, NSA_D, 1)
            k_o[0, g, :, 0:LANES] = first
            k_o[0, g, :, LANES:2 * LANES] = jnp.where(lane < NSA_D, kk, kextra).astype(BF16)
            v_o[0, g, :, 0:LANES] = jnp.where(lane < NSA_D, v_lo, 1.0).astype(BF16)
            v_o[0, g, :, LANES:2 * LANES] = jnp.where(lane < NSA_D, 1.0, v_hi).astype(BF16)

    gs = _sigmoid(proj("gnsa"))
    for g in range(NSA_GROUPS):
        gs_o[0, g] = gs[:, LANES * g:LANES * (g + 1)]

    ctab = ctab_ref[...]
    stab = stab_ref[...]
    scale = (MLA_NOPE + MLA_ROPE) ** -0.5 * LOG2E
    cqn =(_rms(proj("cq")) * qn_ref[...]).astype(BF16)
    q_a = _dot(cqn, wqa_ref[...])
    q_b = _dot(cqn, wqb_ref[...])
    ckvn = (_rms(proj("ckv")) * kvn_ref[...]).astype(BF16)
    kn = _dot(ckvn, wuk_ref[...])
    vv = _dot(ckvn, wuv_ref[...])
    kr = proj("kr")
    krope = kr[:, :LANES] * ctab + kr[:, LANES:] * stab
    for h in range(MLA_HEADS):
        sl = slice(LANES * h, LANES * (h + 1))
        qcat_o[0, :, sl] = ((q_a[:, sl] * ctab + q_b[:, sl] * stab) * scale).astype(BF16)
        kcat_o[0, :, sl] = (kn[:, sl] + krope).astype(BF16)
        vcat_o[0, :, sl] = jnp.where(lane == MLA_V, 1.0, vv[:, sl]).astype(BF16)

    gm_o[0] = _sigmoid(proj("gm"))


def _inproj(x, sh, sc, w_all, wqa, wqb, wuk, wuv, qn, kvn, ctab, stab, *, tm=512):
    B, S, D = x.shape
    G = NSA_GROUPS
    grid = (B, S // tm)
    vec = pl.BlockSpec((1, 1, D), lambda b, i: (b, 0, 0))

    def res(a):
        return _resident(a.shape, lambda b, i: (0,) * a.ndim)

    row = lambda w: pl.BlockSpec((1, tm, w), lambda b, i: (b, i, 0))
    grp = lambda w: pl.BlockSpec((1, G, tm, w), lambda b, i: (b, 0, i, 0))
    tab = pl.BlockSpec((tm, LANES), lambda b, i: (i, 0))
    out_shape = (
        jax.ShapeDtypeStruct((B, S, 512), BF16),
        jax.ShapeDtypeStruct((B, 2, S, LANES), F32),
        jax.ShapeDtypeStruct((B, G, S, 256), BF16),
        jax.ShapeDtypeStruct((B, G, S, 256), BF16),
        jax.ShapeDtypeStruct((B, G, S, 256), BF16),
        jax.ShapeDtypeStruct((B, G, S, 256), BF16),
        jax.ShapeDtypeStruct((B, G, S, 128), F32),
        jax.ShapeDtypeStruct((B, S, MLA_HEADS * LANES), BF16),
        jax.ShapeDtypeStruct((B, S, MLA_HEADS * LANES), BF16),
        jax.ShapeDtypeStruct((B, S, MLA_HEADS * LANES), BF16),
        jax.ShapeDtypeStruct((B, S, 2048), F32),
    )
    out_specs = (row(512), grp(LANES), grp(256), grp(256), grp(256), grp(256), grp(128),
                 row(MLA_HEADS * LANES), row(MLA_HEADS * LANES), row(MLA_HEADS * LANES), row(2048))
    return pl.pallas_call(
        functools.partial(_inproj_kernel, tm=tm),
        grid=grid,
        in_specs=[pl.BlockSpec((1, tm, D), lambda b, i: (b, i, 0)), vec, vec,
                  res(w_all), res(wqa), res(wqb), res(wuk), res(wuv), res(qn), res(kvn), tab, tab],
        out_specs=out_specs,
        out_shape=out_shape,
        compiler_params=_params("parallel", "parallel"),
        name="in_proj",
    )(x, sh, sc, w_all, wqa, wqb, wuk, wuv, qn, kvn, ctab, stab)


def _compress_kernel(a_ref, pea_ref, peb_ref, wa_ref, wb_ref, w2_ref, kcat_o, vct_o):
    planes, s_len, w = a_ref.shape[1:]
    n = s_len // CMP_STRIDE
    pa = jnp.zeros((n, planes * w), F32)
    pb = jnp.zeros((n, planes * w), F32)
    for t in range(CMP_STRIDE):
        a = jnp.concatenate([a_ref[0, j, pl.ds(t, n, stride=CMP_STRIDE), :] for j in range(planes)], axis=1)
        ws = slice(planes * w * t, planes * w * (t + 1))
        pa = pa + _dot((a + pea_ref[:, ws]).astype(BF16), wa_ref[ws, :])
        pb = pb + _dot((a + peb_ref[:, ws]).astype(BF16), wb_ref[ws, :])
    h = pa + pltpu.roll(pb, n - 1, 0)
    h = 0.5 * h * (1.0 + jnp.tanh(0.7978845608028654 * (h + 0.044715 * (h * h * h))))
    out = _dot(h.astype(BF16), w2_ref[...])
    out_t = out.T
    lane = lax.broadcasted_iota(jnp.int32, (n, LANES), 1)
    row = lax.broadcasted_iota(jnp.int32, (n, LANES), 0)
    pair = jnp.right_shift(row, 1)
    within = (row - 2 * pair) * CMP_STRIDE + (CMP_BLOCK - 1)
    kextra = jnp.where(lane < NSA_D + N_PIECES, pair.astype(F32),
                       jnp.where(lane < NSA_D + 2 * N_PIECES, within.astype(F32), 0.0))
    kk = out[:, 0:LANES] * (NSA_D ** -0.5 * LOG2E)
    for g in range(NSA_GROUPS):
        x = kk if g == 0 else pltpu.roll(kk, NSA_D, 1)
        lo = x - x.astype(BF16).astype(F32)
        kcat_o[0, g, :, 0:LANES] = jnp.where(lane < NSA_D, x, kextra).astype(BF16)
        kcat_o[0, g, :, LANES:2 * LANES] = jnp.where(lane < NSA_D, lo, 0.0).astype(BF16)
        vct_o[0, g] = out_t[2 * NSA_D + NSA_D * g:2 * NSA_D + NSA_D * (g + 1), :].astype(BF16)


def _compress(kcvc, pea, peb, wa, wb, w2):
    B, P, S, W = kcvc.shape
    n = S // CMP_STRIDE
    G = NSA_GROUPS
    full = lambda arr: _resident(arr.shape, lambda b: (0,) * arr.ndim)
    return pl.pallas_call(
        _compress_kernel,
        grid=(B,),
        in_specs=[pl.BlockSpec((1, P, S, W), lambda b: (b, 0, 0, 0)),
                  full(pea), full(peb), full(wa), full(wb), full(w2)],
        out_specs=(pl.BlockSpec((1, G, n, 2 * LANES), lambda b: (b, 0, 0, 0)),
                   pl.BlockSpec((1, G, NSA_D, n), lambda b: (b, 0, 0, 0))),
        out_shape=(jax.ShapeDtypeStruct((B, G, n, 2 * LANES), BF16),
                   jax.ShapeDtypeStruct((B, G, NSA_D, n), BF16)),
        compiler_params=_params("parallel"),
        name="compress",
    )(kcvc, pea, peb, wa, wb, w2)


def _nsa_select_kernel(q_ref, kcat_ref, vct_ref, oc_o, qaug_o, used_o, qc_scr, s_scr, p_scr, *, n_top):
    g = pl.program_id(1)
    i = pl.program_id(2)
    tq = Q_BLOCK
    qb = q_ref.shape[1] // tq
    tqs = qb * tq
    ncp = kcat_ref.shape[2]
    q0 = i * tqs
    scale = NSA_D ** -0.5

    lane = lax.broadcasted_iota(jnp.int32, (tq, LANES), 1)
    for c in range(qb):
        qf = q_ref[0, tq * c:tq * (c + 1), :].astype(F32)
        for h in range(NSA_HPG):
            rs = slice(tq * h, tq * (h + 1))
            x = qf[:, LANES * (h // 2):LANES * (h // 2 + 1)]
            if h % 2 == 1:
                x = pltpu.roll(x, NSA_D, 1)
            extra = jnp.zeros((tq, LANES), F32)
            extra_c = jnp.zeros((tq, LANES), F32)
            for k in range(N_PIECES):
                piece = jnp.where(g == 0, SLOPE_PIECES[h][k], SLOPE_PIECES[NSA_HPG + h][k])
                extra = jnp.where(lane == NSA_D + k, piece,
                                  jnp.where(lane == NSA_D + N_PIECES + k, piece * SLC_BLOCK, extra))
                extra_c = jnp.where(lane == NSA_D + k, piece * (2 * CMP_STRIDE),
                                    jnp.where(lane == NSA_D + N_PIECES + k, piece, extra_c))
            qaug_o[0, 0, c, rs, LANES:2 * LANES] = jnp.where(lane < NSA_D, x * (scale * LOG2E), extra).astype(BF16)
            rc = slice(NSA_HPG * tq * c + tq * h, NSA_HPG * tq * c + tq * (h + 1))
            qc_scr[rc, 0:LANES] = jnp.where(lane < NSA_D, x, extra_c).astype(BF16)
            qc_scr[rc, LANES:2 * LANES] = jnp.where(lane < NSA_D, x, 0.0).astype(BF16)

    s_scr[...] = _dot_nt(kcat_ref[0, 0], qc_scr[...])
    cmp_end = lax.broadcasted_iota(jnp.int32, (ncp, tq), 0) * CMP_STRIDE + (CMP_BLOCK - 1)
    psums, inv = [], []
    for c in range(qb):
        mask_c = cmp_end <= q0 + tq * c + lax.broadcasted_iota(jnp.int32, (ncp, tq), 1)
        psum = jnp.zeros((ncp, tq), F32)
        for h in range(NSA_HPG):
            cs = slice(tq * (NSA_HPG * c + h), tq * (NSA_HPG * c + h + 1))
            s = jnp.where(mask_c, s_scr[:, cs], -jnp.inf)
            m = jnp.max(s, axis=0, keepdims=True)
            m = jnp.where(m == -jnp.inf, 0.0, m)
            p = jnp.exp2(s - m)
            r = 1.0 / jnp.maximum(jnp.sum(p, axis=0, keepdims=True), 1.0)
            p_scr[:, cs] = p.astype(BF16)
            psum = psum + p * r
            inv.append(r)
        psums.append(psum)
    oc_t = _dot(vct_ref[0, 0], p_scr[...])
    for c in range(qb):
        oc_o[0, tq * c:tq * (c + 1), :] = jnp.concatenate(
            [oc_t[:, tq * (NSA_HPG * c + h):tq * (NSA_HPG * c + h + 1)] * inv[NSA_HPG * c + h]
             for h in range(NSA_HPG)], axis=0).T
    psum = jnp.concatenate(psums, axis=1)
    tq = tqs

    ratio = SLC_BLOCK // CMP_STRIDE
    r = CMP_BLOCK // CMP_STRIDE
    bb = lax.broadcasted_iota(jnp.int32, (LANES, ncp), 0)
    mm = lax.broadcasted_iota(jnp.int32, (LANES, ncp), 1)
    a_t = jnp.where((mm >= ratio * bb - (r - 1)) & (mm <= ratio * bb + ratio - 1), 1.0, 0.0).astype(BF16)
    p_hi = psum.astype(BF16)
    p_lo = (psum - p_hi.astype(F32)).astype(BF16)
    imp = _dot(a_t, p_hi) + _dot(a_t, p_lo)

    blk = lax.broadcasted_iota(jnp.int32, (LANES, tq), 0)
    t_s = q0 + lax.broadcasted_iota(jnp.int32, (LANES, tq), 1)
    cur = jnp.right_shift(t_s, SLC_SHIFT)
    forced = (blk == 0) | (blk == cur) | (blk == cur - 1)
    valid = blk * SLC_BLOCK <= t_s
    cur_q = jnp.right_shift(q0 + lax.broadcasted_iota(jnp.int32, (1, tq), 1), SLC_SHIFT)
    n_forced = 1 + jnp.where(cur_q >= 1, 1, 0) + jnp.where(cur_q >= 2, 1, 0)
    left = (n_top - n_forced).astype(F32)
    cand = jnp.where(forced, -jnp.inf, jnp.where(valid, imp, -jnp.inf))
    blkf = blk.astype(F32)

    def pick_one(r, carry):
        cand, sel = carry
        m = jnp.max(cand, axis=0, keepdims=True)
        idx = jnp.min(jnp.where(cand == m, blkf, float(LANES)), axis=0, keepdims=True)
        pick = blkf == jnp.where(left > r.astype(F32), idx, -1.0)
        return jnp.where(pick, -jnp.inf, cand), jnp.where(pick, 1.0, sel)

    n_rounds = jnp.where(q0 >= 2 * SLC_BLOCK, n_top - 3, n_top - 1)
    _, sel = lax.fori_loop(0, jnp.maximum(n_rounds, 0), pick_one, (cand, jnp.where(forced, 1.0, 0.0)))
    pen_t = jnp.where(valid, jnp.where(sel > 0.0, 0.0, NEG), NEG)
    pen = pen_t.T
    for c in range(qb):
        pen_c = pen[Q_BLOCK * c:Q_BLOCK * (c + 1)]
        for h in range(NSA_HPG):
            qaug_o[0, 0, c, Q_BLOCK * h:Q_BLOCK * (h + 1), 0:LANES] = pen_c.astype(BF16)
        used_o[0, 0, c] = jnp.where(jnp.max(pen_c, axis=0, keepdims=True) == 0.0, 1.0, 0.0)


def _nsa_select(qa, kc, vct, *, n_top, qb=4):
    B, S, _ = qa.shape
    G = NSA_GROUPS
    tq = Q_BLOCK
    nq = S // tq
    ncp = kc.shape[2]
    gw = NSA_HPG * NSA_D
    rows = qb * NSA_HPG * tq
    return pl.pallas_call(
        functools.partial(_nsa_select_kernel, n_top=n_top),
        grid=(B, G, nq // qb),
        in_specs=[pl.BlockSpec((1, qb * tq, gw), lambda b, g, i: (b, i, g)),
                  pl.BlockSpec((1, 1, ncp, 2 * LANES), lambda b, g, i: (b, g, 0, 0)),
                  pl.BlockSpec((1, 1, NSA_D, ncp), lambda b, g, i: (b, g, 0, 0))],
        out_specs=(pl.BlockSpec((1, qb * tq, gw), lambda b, g, i: (b, i, g)),
                   pl.BlockSpec((1, 1, qb, NSA_HPG * tq, 2 * LANES), lambda b, g, i: (b, g, i, 0, 0)),
                   pl.BlockSpec((1, 1, qb, 1, LANES), lambda b, g, i: (b, g, i, 0, 0))),
        out_shape=(jax.ShapeDtypeStruct((B, S, G * gw), F32),
                   jax.ShapeDtypeStruct((B, G, nq, NSA_HPG * tq, 2 * LANES), BF16),
                   jax.ShapeDtypeStruct((B, G, nq, 1, LANES), F32)),
        scratch_shapes=[pltpu.VMEM((rows, 2 * LANES), BF16),
                        pltpu.VMEM((ncp, rows), F32), pltpu.VMEM((ncp, rows), BF16)],
        compiler_params=_params("parallel", "parallel", "parallel"),
        name="nsa_select",
    )(qa, kc, vct)


CHUNK = 64


def _scores(q, k, buf, row0=0):
    buf[0][row0:, :k.shape[0]] = _dot_nt(q[row0:], k)


def _softmax_pv(buf, v, m_scr, acc_scr, bias_fn=None, bias_key=None, row0=0):
    s_scr, p_scr = buf
    rows, n = m_scr.shape[0], v.shape[0]
    biases = {}
    for r0 in range(row0, rows, CHUNK):
        rs = slice(r0, r0 + CHUNK)
        s = s_scr[rs, :n]
        if bias_fn is not None:
            key = r0 if bias_key is None else bias_key(r0)
            if key not in biases:
                biases[key] = bias_fn(r0)
            if biases[key] is not None:
                s = s + biases[key]
        m_old = m_scr[rs]
        m_new = jnp.maximum(m_old, jnp.broadcast_to(jnp.max(s, axis=-1, keepdims=True), m_old.shape))
        m_scr[rs] = m_new
        alpha = jnp.exp2(m_old - m_new)
        for c in range(0, acc_scr.shape[1], LANES):
            acc_scr[rs, c:c + LANES] = acc_scr[rs, c:c + LANES] * alpha
        for c in range(0, n, LANES):
            p_scr[rs, c:c + LANES] = jnp.exp2((s[:, c:c + LANES] - m_new).astype(BF16))
    acc_scr[row0:] += _dot(p_scr[row0:, :n], v)


def _causal_flash(chains, n_full, tk, tail, bias_key=None, overlap_fn=None, n_full_even=False,
                  tile_at=lambda p: p):
    off = lambda p: pl.multiple_of(tile_at(p) * tk, tk)
    n_pairs = lax.div(n_full, 2)

    def scores(p, which, row0=0):
        for q, k_fn, _, buf_a, buf_b, _, _ in chains:
            _scores(q, k_fn(off(p)), (buf_a, buf_b)[which], row0=row0)

    def softmax_pv(p, which, bias_fn=None, row0=0):
        for _, _, v_fn, buf_a, buf_b, m_scr, acc_scr in chains:
            _softmax_pv((buf_a, buf_b)[which], v_fn(off(p)), m_scr, acc_scr, bias_fn=bias_fn, bias_key=bias_key,
                        row0=row0)

    def finish(j, steps):
        for k, (row0, bias_fn) in enumerate(steps):
            if k + 1 < len(steps):
                scores(j + k + 1, (k + 1) % 2, row0=steps[k + 1][0])
            softmax_pv(j + k, k % 2, bias_fn=bias_fn, row0=row0)

    scores(0, 0)
    if overlap_fn is not None:
        overlap_fn()

    def pair(jj, carry):
        j = 2 * jj
        scores(j + 1, 1)
        softmax_pv(j, 0)
        scores(j + 2, 0)
        softmax_pv(j + 1, 1)
        return carry

    lax.fori_loop(0, n_pairs, pair, 0)
    j = 2 * n_pairs
    if n_full_even:
        finish(j, tail)
    else:
        pl.when(j < n_full)(lambda: finish(j, [(0, None)] + tail))
        pl.when(j == n_full)(lambda: finish(j, tail))


def _causal_bias(t0, pos0, n):
    t = t0 + lax.broadcasted_iota(jnp.int32, (CHUNK, n), 0)
    pos = pos0 + lax.broadcasted_iota(jnp.int32, (CHUNK, n), 1)
    return jnp.where(pos <= t, 0.0, NEG)


def _nsa_attend_kernel(used_ref, qaug_ref, ks_ref, vs_ref, kw_ref, vw_ref, oc_ref, gs_ref, o_ref,
                       s_a, s_b, p_a, p_b, m_scr, mw_scr, accs_scr, accw_scr, tiles_ref, *, tk, n_tiles):
    i = pl.program_id(2)
    step = (pl.program_id(0) * pl.num_programs(1) + pl.program_id(1)) * pl.num_programs(2) + i
    buf_a, buf_b = (s_a, p_a), (s_b, p_b)
    qb = qaug_ref.shape[2]
    blk_rows = NSA_HPG * Q_BLOCK
    rows = qb * blk_rows
    q = qaug_ref[0, 0].reshape(rows, 2 * LANES)
    q0 = i * (qb * Q_BLOCK)
    row_dt = lambda r0: (r0 // blk_rows) * Q_BLOCK + r0 % Q_BLOCK

    offw = jnp.maximum(q0 - WINDOW, 0)
    w_off = lambda w: pl.multiple_of(offw + w * tk, Q_BLOCK)

    def window_bias(w):
        def bias(r0):
            d = (q0 + row_dt(r0) + lax.broadcasted_iota(jnp.int32, (CHUNK, tk), 0)
                 - (w_off(w) + lax.broadcasted_iota(jnp.int32, (CHUNK, tk), 1)))
            return jnp.where(d >= 0, jnp.where(d < WINDOW, 0.0, NEG), NEG)
        return bias

    def window_softmax(w, buf):
        _softmax_pv(buf, vw_ref[0, 0, pl.ds(w_off(w), tk), :], mw_scr, accw_scr,
                    bias_fn=window_bias(w), bias_key=row_dt)

    mw_scr[...] = jnp.full((rows, LANES), NEG, F32)
    accw_scr[...] = jnp.zeros(accw_scr.shape, F32)
    _scores(q, kw_ref[0, 0, pl.ds(w_off(0), tk), :], buf_a)
    _scores(q, kw_ref[0, 0, pl.ds(w_off(1), tk), :], buf_b)
    window_softmax(0, buf_a)

    m_scr[...] = jnp.full((rows, LANES), NEG, F32)
    accs_scr[...] = jnp.zeros(accs_scr.shape, F32)
    n_full = lax.div(q0, tk)
    n_used = jnp.int32(0)
    for t in range(n_tiles):
        tiles_ref[n_used] = t
        n_used = n_used + jnp.where((used_ref[step * n_tiles + t] > 0) & (t < n_full), 1, 0)
    tile_at = lambda p: jnp.where(p < n_used, tiles_ref[jnp.minimum(p, n_tiles)], n_full + p - n_used)
    chain = (q, lambda o: ks_ref[0, 0, pl.ds(o, tk), :], lambda o: vs_ref[0, 0, pl.ds(o, tk), :],
             buf_a, buf_b, m_scr, accs_scr)
    _causal_flash([chain], n_used, tk,
                  tail=[(0, lambda r0: _causal_bias(q0 + row_dt(r0), n_full * tk, tk))], bias_key=row_dt,
                  overlap_fn=lambda: window_softmax(1, buf_b), tile_at=tile_at)

    gw = NSA_HPG * NSA_D
    gs = gs_ref[0, 0]
    src = lax.broadcasted_iota(jnp.int32, (LANES, 3 * gw), 0)
    dst = lax.broadcasted_iota(jnp.int32, (LANES, 3 * gw), 1)
    branch = jnp.right_shift(dst, gw.bit_length() - 1)
    head = jnp.right_shift(dst & (gw - 1), NSA_D.bit_length() - 1)
    spread = jnp.where(src == 3 * head + branch, 1.0, 0.0).astype(BF16)
    gs_hi = gs.astype(BF16)
    gs_lo = (gs - gs_hi.astype(F32)).astype(BF16)
    g_exp = _dot(gs_hi, spread) + _dot(gs_lo, spread)

    lane = lax.broadcasted_iota(jnp.int32, (Q_BLOCK, LANES), 1)

    def normalised(acc_ref, c, h2):
        r0 = blk_rows * c + Q_BLOCK * 2 * h2
        even = acc_ref[r0:r0 + Q_BLOCK, :]
        odd = acc_ref[r0 + Q_BLOCK:r0 + 2 * Q_BLOCK, :]
        return jnp.where(lane < NSA_D, even[:, :LANES] / even[:, LANES:], odd[:, LANES:] / odd[:, :LANES])

    for c in range(qb):
        qs = slice(Q_BLOCK * c, Q_BLOCK * (c + 1))
        for h2 in range(NSA_HPG // 2):
            ls = slice(LANES * h2, LANES * (h2 + 1))
            gate = lambda j: g_exp[qs, gw * j + LANES * h2:gw * j + LANES * (h2 + 1)]
            o = (gate(0) * oc_ref[0, qs, ls] + gate(1) * normalised(accs_scr, c, h2)
                 + gate(2) * normalised(accw_scr, c, h2))
            o_ref[0, qs, ls] = o.astype(BF16)


def _nsa_attend(qaug, used, ks, vs, kw, vw, oc, gs, *, tk=512, qb=4):
    B, G, nq = qaug.shape[:3]
    S = ks.shape[2]
    tq = qb * Q_BLOCK
    tk = min(tk, S)
    gw = NSA_HPG * NSA_D
    rows = NSA_HPG * tq
    nw = WINDOW + tq
    assert S >= nw and nq % qb == 0 and tk % tq == 0 and nw == 2 * tk
    kv = lambda w: pl.BlockSpec((1, 1, S, w), lambda b, g, i, u: (b, g, 0, 0), pipeline_mode=pl.Buffered(1))
    n_tiles = LANES * SLC_BLOCK // tk
    used = used.reshape(B, G, nq // qb, qb, n_tiles, tk // SLC_BLOCK).max(axis=(3, 5))
    used = (used > 0).astype(jnp.int32).reshape(-1)
    return pl.pallas_call(
        functools.partial(_nsa_attend_kernel, tk=tk, n_tiles=n_tiles),
        grid_spec=pltpu.PrefetchScalarGridSpec(
            num_scalar_prefetch=1,
            grid=(B, G, nq // qb),
            in_specs=[pl.BlockSpec((1, 1, qb, NSA_HPG * Q_BLOCK, 2 * LANES), lambda b, g, i, u: (b, g, i, 0, 0)),
                      kv(2 * LANES), kv(2 * LANES), kv(2 * LANES), kv(2 * LANES),
                      pl.BlockSpec((1, tq, gw), lambda b, g, i, u: (b, i, g)),
                      pl.BlockSpec((1, 1, tq, LANES), lambda b, g, i, u: (b, g, i, 0))],
            out_specs=pl.BlockSpec((1, tq, gw), lambda b, g, i, u: (b, i, g)),
            scratch_shapes=[pltpu.VMEM((rows, tk), F32), pltpu.VMEM((rows, tk), F32),
                            pltpu.VMEM((rows, tk), BF16), pltpu.VMEM((rows, tk), BF16),
                            pltpu.VMEM((rows, LANES), F32), pltpu.VMEM((rows, LANES), F32),
                            pltpu.VMEM((rows, 2 * LANES), F32), pltpu.VMEM((rows, 2 * LANES), F32),
                            pltpu.SMEM((n_tiles + 1,), jnp.int32)]),
        out_shape=jax.ShapeDtypeStruct((B, S, G * gw), BF16),
        compiler_params=_params("parallel", "parallel", "arbitrary"),
        name="nsa_attend",
    )(used, qaug, ks, vs, kw, vw, oc, gs)


def _mla_kernel(q_ref, k_ref, v_ref, o_ref, *scratch, tk, hps):
    i = pl.program_id(2)
    tail = [(0, lambda r0: _causal_bias(r0, 0, tk) if r0 < tk else None),
            (tk, lambda r0: _causal_bias(r0 - tk, 0, tk))]
    chains = []
    for h in range(hps):
        s_a, s_b, p_a, p_b, m_scr, acc_scr = scratch[6 * h:6 * (h + 1)]
        ls = slice(LANES * h, LANES * (h + 1))
        m_scr[...] = jnp.full(m_scr.shape, NEG, F32)
        acc_scr[...] = jnp.zeros(acc_scr.shape, F32)
        chains.append((q_ref[0, :, ls],
                       lambda o, ls=ls: k_ref[0, pl.ds(o, tk), ls], lambda o, ls=ls: v_ref[0, pl.ds(o, tk), ls],
                       (s_a, p_a), (s_b, p_b), m_scr, acc_scr))
    _causal_flash(chains, 2 * i, tk, tail=tail, n_full_even=True)
    for h in range(hps):
        acc = scratch[6 * h + 5][...]
        o_ref[0, :, LANES * h:LANES * (h + 1)] = (acc / acc[:, MLA_V:MLA_V + 1]).astype(BF16)


def _mla_attend(qcat, kcat, vcat, *, tk=512, hps=2):
    B, S, _ = qcat.shape
    tk = min(tk, S // 2)
    t = 2 * tk
    H = MLA_HEADS
    wl = hps * LANES
    kv = pl.BlockSpec((1, S, wl), lambda b, h, i: (b, 0, h))
    per_head = [pltpu.VMEM((t, tk), F32), pltpu.VMEM((t, tk), F32),
                pltpu.VMEM((t, tk), BF16), pltpu.VMEM((t, tk), BF16),
                pltpu.VMEM((t, LANES), F32), pltpu.VMEM((t, LANES), F32)]
    return pl.pallas_call(
        functools.partial(_mla_kernel, tk=tk, hps=hps),
        grid=(B, H // hps, S // t),
        in_specs=[pl.BlockSpec((1, t, wl), lambda b, h, i: (b, i, h)), kv, kv],
        out_specs=pl.BlockSpec((1, t, wl), lambda b, h, i: (b, i, h)),
        out_shape=jax.ShapeDtypeStruct((B, S, H * LANES), BF16),
        scratch_shapes=per_head * hps,
        compiler_params=_params("parallel", "parallel", "arbitrary"),
        name="mla_attend",
    )(qcat, kcat, vcat)


def _merge_kernel(x_ref, g_ref, gm_ref, oa_ref, ob_ref, wa_ref, wb_ref, wo_ref, o_ref):
    D = x_ref.shape[2]
    gm = gm_ref[0]
    y = gm[:, :D] * _dot(oa_ref[0], wa_ref[...]) + gm[:, D:] * _dot(ob_ref[0], wb_ref[...])
    o_ref[0] = x_ref[0] + g_ref[0] * _dot(y.astype(BF16), wo_ref[...])


def _merge(x, g, gm, oa, ob, wa, wb, wo, *, tm=512):
    B, S, D = x.shape
    row = lambda w: pl.BlockSpec((1, tm, w), lambda b, i: (b, i, 0))
    res = lambda a: _resident(a.shape, lambda b, i: (0, 0))
    return pl.pallas_call(
        _merge_kernel,
        grid=(B, S // tm),
        in_specs=[row(D), pl.BlockSpec((1, 1, D), lambda b, i: (b, 0, 0)), row(2 * D),
                  row(oa.shape[2]), row(ob.shape[2]), res(wa), res(wb), res(wo)],
        out_specs=row(D),
        out_shape=jax.ShapeDtypeStruct((B, S, D), F32),
        compiler_params=_params("parallel", "parallel"),
        name="merge",
    )(x, g, gm, oa, ob, wa, wb, wo)


def _pad_heads(w, n_heads, width):
    k = w.shape[0]
    w = w.reshape(k, n_heads, width)
    return jnp.pad(w, ((0, 0), (0, 0), (0, LANES - width))).reshape(k, n_heads * LANES)


def _rot_cols(w):
    half = w.shape[-1] // 2
    return jnp.concatenate([-w[..., half:], w[..., :half]], axis=-1)


def _prep_inproj_weights(w_in, w_uq, w_uk, w_uv):
    D = w_in.shape[0]
    o = 0
    seg = {}
    for name, width in (("qa", 512), ("kc", 128), ("vc", 128), ("ks", 128), ("vs", 128), ("kw", 128),
                        ("vw", 128), ("gnsa", 3 * NSA_HEADS), ("cq", MLA_Q_RANK), ("ckv", MLA_KV_RANK),
                        ("kr", MLA_ROPE), ("gm", 2048)):
        seg[name] = w_in[:, o:o + width]
        o += width
    per_g = 3 * NSA_HPG
    gn = jnp.concatenate(
        [jnp.pad(seg["gnsa"][:, per_g * g:per_g * (g + 1)], ((0, 0), (0, LANES - per_g)))
         for g in range(NSA_GROUPS)], axis=1)
    z64 = jnp.zeros((D, MLA_NOPE), F32)
    z32 = jnp.zeros((D, LANES - MLA_NOPE - MLA_ROPE), F32)
    kr = jnp.concatenate([z64, seg["kr"], z32, z64, _rot_cols(seg["kr"]), z32], axis=1)
    w_all = jnp.concatenate([seg["qa"], seg["kc"], seg["vc"], seg["ks"], seg["vs"], seg["kw"], seg["vw"],
                             gn, seg["cq"], seg["ckv"], kr, seg["gm"]], axis=1).astype(BF16)
    assert w_all.shape[1] == _SEG_TOTAL
    hd = MLA_NOPE + MLA_ROPE
    uq = w_uq.reshape(-1, MLA_HEADS, hd)
    uq_rot = jnp.concatenate([jnp.zeros_like(uq[..., :MLA_NOPE]), _rot_cols(uq[..., MLA_NOPE:])], axis=-1)
    wqa = _pad_heads(w_uq, MLA_HEADS, hd).astype(BF16)
    wqb = _pad_heads(uq_rot.reshape(-1, MLA_HEADS * hd), MLA_HEADS, hd).astype(BF16)
    wuk = _pad_heads(w_uk, MLA_HEADS, MLA_NOPE).astype(BF16)
    wuv = _pad_heads(w_uv, MLA_HEADS, MLA_V).astype(BF16)
    return w_all, wqa, wqb, wuk, wuv


def _prep_compress_weights(k_w1, k_w2, k_pe, v_w1, v_w2, v_pe):
    d = NSA_D
    half = CMP_STRIDE

    def blockdiag(mats):
        n = len(mats)
        z = jnp.zeros_like(mats[0])
        rows = [jnp.concatenate([mats[r] if c == r else z for c in range(n)], axis=-1) for r in range(n)]
        return jnp.concatenate(rows, axis=-2)

    def first(w1, lo):
        return w1.reshape(CMP_BLOCK, d, d)[lo:lo + half]

    def w1_half(lo):
        k, v = first(k_w1, lo), first(v_w1, lo)
        return blockdiag([k, k, v, v]).reshape(half * 4 * d, 4 * d).astype(BF16)

    def pe_half(lo):
        k, v = k_pe[lo:lo + half], v_pe[lo:lo + half]
        return jnp.concatenate([k, k, v, v], axis=-1).reshape(1, half * 4 * d)

    w2 = blockdiag([k_w2, k_w2, v_w2, v_w2]).astype(BF16)
    return pe_half(0), pe_half(half), w1_half(0), w1_half(half), w2


def _rope_tables(S):
    half = MLA_ROPE // 2
    inv_freq = ROPE_THETA ** (-jnp.arange(half, dtype=F32) / half)
    ang = jnp.arange(S, dtype=F32)[:, None] * inv_freq[None, :]
    cos, sin = jnp.cos(ang), jnp.sin(ang)
    ones = jnp.ones((S, MLA_NOPE), F32)
    z_n = jnp.zeros((S, MLA_NOPE), F32)
    z_p = jnp.zeros((S, LANES - MLA_NOPE - MLA_ROPE), F32)
    ctab = jnp.concatenate([ones, cos, cos, z_p], axis=1)
    stab = jnp.concatenate([z_n, sin, sin, z_p], axis=1)
    return ctab, stab


def kernel(x, c, w_ada, b_ada, ffn1_gate, ffn1_up, ffn1_down, ffn2_gate, ffn2_up, ffn2_down, w_in, cmpk_w1, cmpk_w2, cmpk_pe, cmpv_w1, cmpv_w2, cmpv_pe, mla_q_norm, mla_w_uq, mla_kv_norm, mla_w_uk, mla_w_uv, w_branch_a, w_branch_b, w_out, final_norm):
    B, S, D = x.shape
    L = w_ada.shape[0]
    assert NSA_GROUPS == 2 and S // SLC_BLOCK <= LANES and S // (2 * CMP_STRIDE) <= 256 and S % 512 == 0
    n_top = min(N_SELECT, S // SLC_BLOCK)
    ctab, stab = _rope_tables(S)
    mod = _adaln(c, w_ada, b_ada).reshape(L, B, N_MOD, 1, D)
    bf = lambda w: w.astype(BF16)
    ffn1 = [_to_bf16(w) for w in (ffn1_gate, ffn1_up, ffn1_down)]
    ffn2 = [_to_bf16(w) for w in (ffn2_gate, ffn2_up, ffn2_down)]
    for l in range(L):
        sh1, sc1, g1, sh2, sc2, g2, sh3, sc3, g3 = (mod[l, :, k] for k in range(N_MOD))
        x = _ffn(x, sh1, sc1, g1, *ffn1, l, final_norm, final=False)

        w_all, wqa, wqb, wuk, wuv = _prep_inproj_weights(w_in[l], mla_w_uq[l], mla_w_uk[l], mla_w_uv[l])
        (qa, kcvc, ks, vs, kw, vw, gs, qcat, kcat, vcat, gm) = _inproj(
            x, sh2, sc2, w_all, wqa, wqb, wuk, wuv,
            mla_q_norm[l].reshape(1, -1), mla_kv_norm[l].reshape(1, -1), ctab, stab)
        kc, vct = _compress(kcvc, *_prep_compress_weights(
            cmpk_w1[l], cmpk_w2[l], cmpk_pe[l], cmpv_w1[l], cmpv_w2[l], cmpv_pe[l]))
        oc, qaug, used = _nsa_select(qa, kc, vct, n_top=n_top)
        oa = _nsa_attend(qaug, used, ks, vs, kw, vw, oc, gs)
        ob = _mla_attend(qcat, kcat, vcat)
        wb_pad = jnp.pad(w_branch_b[l].reshape(MLA_HEADS, MLA_V, D),
                         ((0, 0), (0, LANES - MLA_V), (0, 0))).reshape(MLA_HEADS * LANES, D)
        x = _merge(x, g2, gm, oa, ob, bf(w_branch_a[l]), bf(wb_pad), bf(w_out[l]))

        x = _ffn(x, sh3, sc3, g3, *ffn2, l, final_norm, final=(l == L - 1))
    return x
```
